```python
import jax
import jax.numpy as jnp
from jax import lax
import numpy as np

D_MODEL = 1024
BATCH = 2
SEQ = 8192
DEPTH = 2

EPS = 1e-6
ML_HEADS = 4
ML_QK = 64
ML_V = 128
ML_CONV = 4
ML_CHUNK = 64
MLA_HEADS = 8
MLA_NOPE = 64
MLA_ROPE = 32
MLA_V = 64
MLA_Q_RANK = 256
MLA_KV_RANK = 128
ROPE_THETA = 10000.0
Q_BLOCK = 128
D_FF = 4 * D_MODEL

ML_WIDTH = ML_HEADS * ML_V
MLA_WIDTH = MLA_HEADS * MLA_V
MIX_WIDTH = ML_WIDTH + MLA_WIDTH
QK_COLS = 2 * ML_HEADS * ML_QK
GATE_COLS = 2 * ML_HEADS
IN_WIDTH = QK_COLS + 2 * ML_WIDTH + GATE_COLS + MLA_Q_RANK + MLA_KV_RANK + MLA_ROPE

kernel_name = 'hybrid_mlstm_mla_sandwich'


def rmsnorm(x, g):
    xf = x.astype(jnp.float32)
    y = xf * lax.rsqrt(jnp.mean(xf * xf, axis=-1, keepdims=True) + EPS)
    return (y * g.astype(jnp.float32)).astype(x.dtype)


def rope_tables(positions):
    inv = 1.0 / (ROPE_THETA ** (jnp.arange(0, MLA_ROPE, 2, dtype=jnp.float32) / MLA_ROPE))
    ang = positions.astype(jnp.float32)[..., None] * inv
    return jnp.cos(ang), jnp.sin(ang)


def apply_rope(x, cos, sin):
    xf = x.astype(jnp.float32)
    x1, x2 = jnp.split(xf, 2, axis=-1)
    return jnp.concatenate([x1 * cos - x2 * sin, x1 * sin + x2 * cos], axis=-1).astype(x.dtype)


def causal_conv(x, w, b):
    K = w.shape[0]
    S = x.shape[1]
    xp = jnp.pad(x, ((0, 0), (K - 1, 0), (0, 0)))
    return sum(xp[:, k:k + S] * w[k] for k in range(K)) + b


def mlstm_chunkwise(q, k, v, i_pre, f_pre):
    B, S, H, dk = q.shape
    dv = v.shape[-1]
    L = ML_CHUNK
    NC = S // L
    f32 = jnp.float32

    def chunk(t):
        t = t.reshape((B, NC, L, H) + t.shape[3:])
        return jnp.moveaxis(t, 3, 1)

    qc = chunk(q).astype(f32)
    kc = chunk(k).astype(f32) * (dk ** -0.5)
    vc = chunk(v).astype(f32)
    ig = chunk(i_pre).astype(f32)
    logf = jax.nn.log_sigmoid(chunk(f_pre).astype(f32))
    b = jnp.cumsum(logf, axis=-1)
    b_end = b[..., -1]

    a = b_end[..., None] - b + ig
    m_loc = jnp.max(a, axis=-1)
    w_loc = jnp.exp(a - m_loc[..., None])
    C_loc = jnp.einsum('bhcl,bhcld,bhcle->bhcde', w_loc, kc, vc)
    n_loc = jnp.einsum('bhcl,bhcld->bhcd', w_loc, kc)

    def step(carry, xs):
        C, n, m = carry
        Cl, nl, ml, bl = xs
        m_new = jnp.maximum(bl + m, ml)
        s_prev = jnp.exp(bl + m - m_new)
        s_loc = jnp.exp(ml - m_new)
        C_new = s_prev[..., None, None] * C + s_loc[..., None, None] * Cl
        n_new = s_prev[..., None] * n + s_loc[..., None] * nl
        return (C_new, n_new, m_new), (C, n, m)

    init = (jnp.zeros((B, H, dk, dv), f32), jnp.zeros((B, H, dk), f32), jnp.zeros((B, H), f32))
    xs = (jnp.moveaxis(C_loc, 2, 0), jnp.moveaxis(n_loc, 2, 0),
          jnp.moveaxis(m_loc, 2, 0), jnp.moveaxis(b_end, 2, 0))
    _, (C_prev, n_prev, m_prev) = lax.scan(step, init, xs)
    C_prev = jnp.moveaxis(C_prev, 0, 2)
    n_prev = jnp.moveaxis(n_prev, 0, 2)
    m_prev = jnp.moveaxis(m_prev, 0, 2)

    causal = jnp.tril(jnp.ones((L, L), dtype=bool))
    D = b[..., :, None] - b[..., None, :] + ig[..., None, :]
    D = jnp.where(causal, D, -jnp.inf)
    m_inter = b + m_prev[..., None]
    m_comb = jnp.maximum(jnp.max(D, axis=-1), m_inter)
    Dw = jnp.exp(D - m_comb[..., None])
    inter_w = jnp.exp(m_inter - m_comb)
    s = jnp.einsum('bhcid,bhcjd->bhcij', qc, kc) * Dw
    num = jnp.einsum('bhcij,bhcje->bhcie', s, vc) + inter_w[..., None] * jnp.einsum('bhcid,bhcde->bhcie', qc, C_prev)
    den = jnp.sum(s, axis=-1) + inter_w * jnp.einsum('bhcid,bhcd->bhci', qc, n_prev)
    h = num / jnp.maximum(jnp.abs(den), jnp.exp(-m_comb))[..., None]
    return jnp.moveaxis(h, 1, 3).reshape(B, S, H, dv)


def causal_block_attention(q, k, v):
    B, S, H, dq = q.shape
    dv = v.shape[-1]
    nb = S // Q_BLOCK
    scale = dq ** -0.5
    kf = k.astype(jnp.float32)
    vf = v.astype(jnp.float32)
    qb = jnp.transpose(q.reshape(B, nb, Q_BLOCK, H, dq), (1, 0, 3, 2, 4))
    kpos = jnp.arange(S)

    def one_block(args):
        qblk, bi = args
        sc = jnp.einsum('bhqd,bkhd->bhqk', qblk.astype(jnp.float32), kf) * scale
        qpos = bi * Q_BLOCK + jnp.arange(Q_BLOCK)
        sc = jnp.where(kpos[None, :] <= qpos[:, None], sc, -jnp.inf)
        p = jax.nn.softmax(sc, axis=-1)
        return jnp.einsum('bhqk,bkhd->bqhd', p, vf)

    out = lax.map(one_block, (qb, jnp.arange(nb)))
    return jnp.transpose(out, (1, 0, 2, 3, 4)).reshape(B, S, H * dv)


def token_mixer(a, cos, sin, w_in, b_gates, conv_w, conv_b, ml_head_norm,
                q_norm, w_uq, kv_norm, w_ukv, w_out):
    B, S, _ = a.shape
    proj = a @ w_in
    o0 = QK_COLS
    o1 = o0 + ML_WIDTH
    o2 = o1 + ML_WIDTH
    o3 = o2 + GATE_COLS
    o4 = o3 + MLA_Q_RANK
    o5 = o4 + MLA_KV_RANK
    qk_ml = proj[..., :o0]
    v_ml = proj[..., o0:o1]
    o_ml = proj[..., o1:o2]
    gates = proj[..., o2:o3] + b_gates
    c_q = proj[..., o3:o4]
    c_kv = proj[..., o4:o5]
    k_r = proj[..., o5:]

    qk_ml = jax.nn.silu(causal_conv(qk_ml, conv_w, conv_b))
    q_ml = qk_ml[..., :QK_COLS // 2].reshape(B, S, ML_HEADS, ML_QK)
    k_ml = qk_ml[..., QK_COLS // 2:].reshape(B, S, ML_HEADS, ML_QK)
    h_ml = mlstm_chunkwise(q_ml, k_ml, v_ml.reshape(B, S, ML_HEADS, ML_V),
                           gates[..., :ML_HEADS], gates[..., ML_HEADS:])
    h_ml = rmsnorm(h_ml, ml_head_norm.reshape(ML_HEADS, ML_V))
    h_ml = h_ml * jax.nn.sigmoid(o_ml.astype(jnp.float32)).reshape(B, S, ML_HEADS, ML_V)
    h_ml = h_ml.reshape(B, S, ML_WIDTH).astype(a.dtype)

    q = (rmsnorm(c_q, q_norm) @ w_uq).reshape(B, S, MLA_HEADS, MLA_NOPE + MLA_ROPE)
    q = jnp.concatenate([q[..., :MLA_NOPE],
                         apply_rope(q[..., MLA_NOPE:], cos[:, :, None, :], sin[:, :, None, :])], axis=-1)
    kv = (rmsnorm(c_kv, kv_norm) @ w_ukv).reshape(B, S, MLA_HEADS, MLA_NOPE + MLA_V)
    k_rope = apply_rope(k_r, cos, sin)[:, :, None, :]
    k = jnp.concatenate([kv[..., :MLA_NOPE],
                         jnp.broadcast_to(k_rope, (B, S, MLA_HEADS, MLA_ROPE))], axis=-1)
    h_mla = causal_block_attention(q, k, kv[..., MLA_NOPE:]).astype(a.dtype)

    return jnp.concatenate([h_ml, h_mla], axis=-1) @ w_out


def setup_inputs(seed: int = 0) -> dict:
    key = jax.random.key(seed)
    ks = jax.random.split(key, 20)

    def nrm(k, shape, scale):
        return jax.random.normal(k, shape, jnp.float32) * scale

    def gain(k, shape):
        return 1.0 + 0.05 * jax.random.normal(k, shape, jnp.float32)

    x = nrm(ks[0], (BATCH, SEQ, D_MODEL), 1.0)
    positions = jnp.broadcast_to(jnp.arange(SEQ, dtype=jnp.int32), (BATCH, SEQ))
    b_gates = jnp.concatenate([nrm(ks[3], (DEPTH, ML_HEADS), 0.1),
                               3.0 + nrm(ks[4], (DEPTH, ML_HEADS), 0.1)], axis=-1)
    return {
        'x': x,
        'positions': positions,
        'norm_pre_mix': gain(ks[1], (DEPTH, D_MODEL)),
        'w_in': nrm(ks[2], (DEPTH, D_MODEL, IN_WIDTH), D_MODEL ** -0.5),
        'b_gates': b_gates,
        'conv_w': nrm(ks[5], (DEPTH, ML_CONV, QK_COLS), ML_CONV ** -0.5),
        'conv_b': nrm(ks[6], (DEPTH, QK_COLS), 0.02),
        'ml_head_norm': gain(ks[7], (DEPTH, ML_WIDTH)),
        'q_norm': gain(ks[8], (DEPTH, MLA_Q_RANK)),
        'w_uq': nrm(ks[9], (DEPTH, MLA_Q_RANK, MLA_HEADS * (MLA_NOPE + MLA_ROPE)), MLA_Q_RANK ** -0.5),
        'kv_norm': gain(ks[10], (DEPTH, MLA_KV_RANK)),
        'w_ukv': nrm(ks[11], (DEPTH, MLA_KV_RANK, MLA_HEADS * (MLA_NOPE + MLA_V)), MLA_KV_RANK ** -0.5),
        'w_out': nrm(ks[12], (DEPTH, MIX_WIDTH, D_MODEL), MIX_WIDTH ** -0.5),
        'norm_post_mix': gain(ks[13], (DEPTH, D_MODEL)),
        'norm_pre_mlp': gain(ks[14], (DEPTH, D_MODEL)),
        'w_up': nrm(ks[15], (DEPTH, D_MODEL, D_FF), D_MODEL ** -0.5),
        'w_down': nrm(ks[16], (DEPTH, D_FF, D_MODEL), D_FF ** -0.5),
        'norm_post_mlp': gain(ks[17], (DEPTH, D_MODEL)),
    }


def reference(x, positions, norm_pre_mix, w_in, b_gates, conv_w, conv_b, ml_head_norm,
              q_norm, w_uq, kv_norm, w_ukv, w_out, norm_post_mix, norm_pre_mlp,
              w_up, w_down, norm_post_mlp):
    cos, sin = rope_tables(positions)
    for l in range(DEPTH):
        a = rmsnorm(x, norm_pre_mix[l])
        mix = token_mixer(a, cos, sin, w_in[l], b_gates[l], conv_w[l], conv_b[l], ml_head_norm[l],
                          q_norm[l], w_uq[l], kv_norm[l], w_ukv[l], w_out[l])
        x = x + rmsnorm(mix, norm_post_mix[l])
        m = rmsnorm(x, norm_pre_mlp[l])
        y = jnp.square(jax.nn.relu(m @ w_up[l])) @ w_down[l]
        x = x + rmsnorm(y, norm_post_mlp[l])
    return x
```

```python
import functools

import jax
import jax.numpy as jnp
from jax import lax
from jax.experimental import pallas as pl
from jax.experimental.pallas import tpu as pltpu

F32 = jnp.float32
BF16 = jnp.bfloat16

EPS = 1e-6
ML_HEADS, ML_QK, ML_V, ML_CONV = 4, 64, 128, 4
MLA_HEADS, MLA_NOPE, MLA_ROPE, MLA_V = 8, 64, 32, 64
MLA_Q_RANK, MLA_KV_RANK = 256, 128
ROPE_THETA = 10000.0

ML_WIDTH = ML_HEADS * ML_V
QK_COLS = 2 * ML_HEADS * ML_QK
MLA_WIDTH = MLA_HEADS * MLA_V

LANES = 128
HEAD_BLOCK = LANES
ROPE_LANE0 = MLA_NOPE
GATE_LANE0 = MLA_NOPE + MLA_ROPE
HALF_ROPE = MLA_ROPE // 2
C_QK, C_V, C_O = 0, QK_COLS, QK_COLS + ML_WIDTH
C_CQ = C_O + ML_WIDTH
C_CKV = C_CQ + MLA_Q_RANK
C_BLKA = C_CKV + MLA_KV_RANK
C_BLKB = C_BLKA + LANES
IN_COLS = C_BLKB + LANES

VMEM_LIMIT = 56 * 1024 * 1024


def _params(*sem):
    return pltpu.CompilerParams(dimension_semantics=sem, vmem_limit_bytes=VMEM_LIMIT)


def _rms(x, g):
    return x * lax.rsqrt(jnp.mean(x * x, axis=-1, keepdims=True) + EPS) * g


def _log_sigmoid(x):
    return -(jnp.maximum(-x, 0.0) + jnp.log1p(jnp.exp(-jnp.abs(x))))


def _dot(a, b):
    return jnp.dot(a, b, preferred_element_type=F32)


def _dot_nt(a, b):
    return lax.dot_general(a, b, (((1,), (1,)), ((), ())), preferred_element_type=F32)


def _dot_tn(a, b):
    return lax.dot_general(a, b, (((0,), (0,)), ((), ())), preferred_element_type=F32)


def _rope_table_kernel(pos_ref, inv_ref, cos_ref, sin_ref):
    ang = pos_ref[...].astype(F32) * inv_ref[...]
    lane = lax.broadcasted_iota(jnp.int32, ang.shape, 1)
    rope = (lane >= ROPE_LANE0) & (lane < GATE_LANE0)
    cos_ref[...] = jnp.where(lane < ROPE_LANE0, 1.0, jnp.where(rope, jnp.cos(ang), 0.0))
    sin_ref[...] = jnp.where(rope, jnp.sin(ang), 0.0)


def _rope_tables(positions, tm):
    t = positions.size
    inv = 1.0 / (ROPE_THETA ** (jnp.arange(0, MLA_ROPE, 2, dtype=F32) / MLA_ROPE))
    inv_row = jnp.zeros((1, LANES), F32)
    inv_row = inv_row.at[0, ROPE_LANE0:ROPE_LANE0 + HALF_ROPE].set(inv)
    inv_row = inv_row.at[0, ROPE_LANE0 + HALF_ROPE:GATE_LANE0].set(inv)
    return pl.pallas_call(
        _rope_table_kernel,
        out_shape=(jax.ShapeDtypeStruct((t, LANES), F32),) * 2,
        grid=(t // tm,),
        in_specs=[pl.BlockSpec((tm, 1), lambda i: (i, 0)),
                  pl.BlockSpec((1, LANES), lambda i: (0, 0))],
        out_specs=(pl.BlockSpec((tm, LANES), lambda i: (i, 0)),) * 2,
        compiler_params=_params("parallel"),
        name="rope_tables",
    )(positions.reshape(t, 1), inv_row)


def _in_proj_kernel(x_ref, cos_ref, sin_ref, g_ref, win_ref, gbias_ref, qn_ref, wq_ref, wqs_ref,
                    kvn_ref, wk_ref, wvt_ref,
                    qk_ref, v_ref, o_ref, gb_ref, q_ref, k_ref, vt_ref, *, tk):
    a = _rms(x_ref[...], g_ref[...]).astype(BF16)
    proj = _dot(a, win_ref[...])
    qk_ref[...] = proj[:, C_QK:C_V]
    v_ref[...] = proj[:, C_V:C_O].astype(BF16)
    o_ref[...] = proj[:, C_O:C_CQ]
    blka = proj[:, C_BLKA:C_BLKB]
    blkb = proj[:, C_BLKB:IN_COLS]
    gb_ref[...] = proj[:, C_BLKA:IN_COLS] + gbias_ref[...]

    cos_t = cos_ref[...]
    sin_t = sin_ref[...]
    scale = (MLA_NOPE + MLA_ROPE) ** -0.5

    qn = _rms(proj[:, C_CQ:C_CKV], qn_ref[...]).astype(BF16)
    qf = _dot(qn, wq_ref[...])
    qs = _dot(qn, wqs_ref[...])
    ckvn = _rms(proj[:, C_CKV:C_BLKA], kvn_ref[...]).astype(BF16)
    kf = _dot(ckvn, wk_ref[...])
    lane = lax.broadcasted_iota(jnp.int32, blka.shape, 1)
    rope = (lane >= ROPE_LANE0) & (lane < GATE_LANE0)
    k_rope = jnp.where(rope, blka * cos_t + blkb * sin_t, 0.0)
    for h in range(MLA_HEADS):
        blk = slice(h * HEAD_BLOCK, (h + 1) * HEAD_BLOCK)
        q_ref[:, blk] = ((qf[:, blk] * cos_t + qs[:, blk] * sin_t) * scale).astype(BF16)
        k_ref[:, blk] = (kf[:, blk] + k_rope).astype(BF16)

    vt = _dot_nt(wvt_ref[...], ckvn).astype(BF16)
    for j in range(vt.shape[1] // tk):
        vt_ref[j] = vt[:, j * tk:(j + 1) * tk]


def _in_proj(x2, cos_t, sin_t, g, win, gbias, qn, wq, wqs, kvn, wk, wvt, *, batch, seq, tm, tk):
    t, d = x2.shape
    nt = t // tm
    tiles_per_batch = seq // tm
    kv_per_tile = tm // tk
    row = lambda i: (i, 0)
    const = lambda i: (0, 0)
    full = lambda arr: pl.BlockSpec(arr.shape, const)
    out_shape = (
        jax.ShapeDtypeStruct((t, QK_COLS), F32),
        jax.ShapeDtypeStruct((t, ML_WIDTH), BF16),
        jax.ShapeDtypeStruct((t, ML_WIDTH), F32),
        jax.ShapeDtypeStruct((t, 2 * LANES), F32),
        jax.ShapeDtypeStruct((t, MLA_HEADS * HEAD_BLOCK), BF16),
        jax.ShapeDtypeStruct((t, MLA_HEADS * HEAD_BLOCK), BF16),
        jax.ShapeDtypeStruct((batch, seq // tk, MLA_WIDTH, tk), BF16),
    )
    out_specs = (
        pl.BlockSpec((tm, QK_COLS), row),
        pl.BlockSpec((tm, ML_WIDTH), row),
        pl.BlockSpec((tm, ML_WIDTH), row),
        pl.BlockSpec((tm, 2 * LANES), row),
        pl.BlockSpec((tm, MLA_HEADS * HEAD_BLOCK), row),
        pl.BlockSpec((tm, MLA_HEADS * HEAD_BLOCK), row),
        pl.BlockSpec((None, kv_per_tile, MLA_WIDTH, tk),
                     lambda i: (i // tiles_per_batch, i % tiles_per_batch, 0, 0)),
    )
    in_specs = [pl.BlockSpec((tm, d), row), pl.BlockSpec((tm, LANES), row), pl.BlockSpec((tm, LANES), row),
                full(g), full(win), full(gbias), full(qn), full(wq), full(wqs), full(kvn), full(wk), full(wvt)]
    return pl.pallas_call(
        functools.partial(_in_proj_kernel, tk=tk),
        out_shape=out_shape, grid=(nt,), in_specs=in_specs, out_specs=out_specs,
        compiler_params=_params("parallel"), name="in_proj",
    )(x2, cos_t, sin_t, g, win, gbias, qn, wq, wqs, kvn, wk, wvt)


def _scan_rows(x, op, identity):
    n = x.shape[0]
    row = lax.broadcasted_iota(jnp.int32, x.shape, 0)
    step = 1
    while step < n:
        x = op(x, jnp.where(row >= step, pltpu.roll(x, step, axis=0), identity))
        step *= 2
    return x


def _mlstm_kernel(qk_ref, v_ref, o_ref, gb_ref, cw_ref, cb_ref, hn_ref, out_ref, xbuf, c_ref, m_ref, *, chunk):
    L = chunk
    halo = 8

    @pl.when(pl.program_id(1) == 0)
    def _():
        xbuf[0:halo, :] = jnp.zeros((halo, QK_COLS), F32)
        c_ref[...] = jnp.zeros_like(c_ref)
        m_ref[...] = jnp.zeros_like(m_ref)

    xbuf[halo:halo + L, :] = qk_ref[...]
    cw = cw_ref[...]
    y = cb_ref[...]
    for tap in range(ML_CONV):
        back = ML_CONV - 1 - tap
        y = y + xbuf[halo - back:halo - back + L, :] * cw[tap:tap + 1, :]
    xbuf[0:halo, :] = xbuf[L:L + halo, :]
    qk = y * jax.nn.sigmoid(y)
    q_all = qk[:, :ML_HEADS * ML_QK]
    k_all = qk[:, ML_HEADS * ML_QK:] * (ML_QK ** -0.5)

    gb = gb_ref[...]
    ig = gb[:, :LANES]
    b = _scan_rows(_log_sigmoid(gb[:, LANES:]), jnp.add, 0.0)
    g = ig - b
    m_prev = m_ref[...]
    big_m = jnp.maximum(_scan_rows(g, jnp.maximum, -jnp.inf), m_prev)
    inter_w = jnp.exp(m_prev - big_m)
    floor = jnp.exp(-(b + big_m))
    m_end = big_m[L - 1:L, :]
    w_state = jnp.exp(g - m_end)
    m_ref[...] = b[L - 1:L, :] + m_end
    g_t = g.T

    row = lax.broadcasted_iota(jnp.int32, (L, L), 0)
    col = lax.broadcasted_iota(jnp.int32, (L, L), 1)
    causal = row >= col
    lane = lax.broadcasted_iota(jnp.int32, (L, LANES), 1)
    ones_blk = jnp.where(lane == 0, 1.0, 0.0).astype(BF16)

    for h in range(ML_HEADS):
        pair, lo = h // 2, (h % 2) * ML_QK
        gl = GATE_LANE0 + h
        blk = slice(pair * LANES, (pair + 1) * LANES)
        mine = (lane >= lo) & (lane < lo + ML_QK)
        qm = jnp.where(mine, q_all[:, blk], 0.0).astype(BF16)
        k_blk = k_all[:, blk]
        s = _dot_nt(qm, k_blk.astype(BF16))
        dw = jnp.where(causal, jnp.exp(g_t[gl:gl + 1, :] - big_m[:, gl:gl + 1]), 0.0)
        v_ext = jnp.concatenate([v_ref[:, h * ML_V:(h + 1) * ML_V], ones_blk], axis=1)
        c_pair = c_ref[pair]
        num = _dot((s * dw).astype(BF16), v_ext) + inter_w[:, gl:gl + 1] * _dot(qm, c_pair.astype(BF16))
        den = num[:, ML_V:ML_V + 1]
        hh = num[:, :ML_V] / jnp.maximum(jnp.abs(den), floor[:, gl:gl + 1])
        hs = slice(h * ML_V, (h + 1) * ML_V)
        hh = _rms(hh, hn_ref[:, hs]) * jax.nn.sigmoid(o_ref[:, hs])
        out_ref[:, hs] = hh.astype(out_ref.dtype)

        kw = (k_blk * w_state[:, gl:gl + 1]).astype(BF16)
        upd = _dot_tn(kw, v_ext)
        rows = slice(lo, lo + ML_QK)
        c_ref[pair, rows, :] = inter_w[L - 1:L, gl:gl + 1] * c_pair[rows, :] + upd[rows, :]


def _mlstm(qk, v, o, gb, conv_w, conv_b, head_norm, *, batch, seq, chunk):
    t = qk.shape[0]
    nc = seq // chunk
    row = lambda b, c: (b * nc + c, 0)
    const = lambda b, c: (0, 0)
    return pl.pallas_call(
        functools.partial(_mlstm_kernel, chunk=chunk),
        out_shape=jax.ShapeDtypeStruct((t, ML_WIDTH), BF16),
        grid=(batch, nc),
        in_specs=[pl.BlockSpec((chunk, QK_COLS), row), pl.BlockSpec((chunk, ML_WIDTH), row),
                  pl.BlockSpec((chunk, ML_WIDTH), row), pl.BlockSpec((chunk, 2 * LANES), row),
                  pl.BlockSpec(conv_w.shape, const), pl.BlockSpec(conv_b.shape, const),
                  pl.BlockSpec(head_norm.shape, const)],
        out_specs=pl.BlockSpec((chunk, ML_WIDTH), row),
        scratch_shapes=[pltpu.VMEM((chunk + 8, QK_COLS), F32),
                        pltpu.VMEM((ML_HEADS // 2, LANES, 2 * LANES), F32),
                        pltpu.VMEM((1, LANES), F32)],
        compiler_params=_params("parallel", "arbitrary"), name="mlstm",
    )(qk, v, o, gb, conv_w, conv_b, head_norm)


def _attn_kernel(q_ref, k_ref, vt_ref, out_ref, *, tq):
    tk = tq
    qi = pl.program_id(2)
    ones_rows = jnp.where(lax.broadcasted_iota(jnp.int32, (16, tk), 0) == 0, 1.0, 0.0).astype(BF16)
    qs = [q_ref[:, h * HEAD_BLOCK:(h + 1) * HEAD_BLOCK] for h in range(2)]

    def step(j, carry, masked):
        new = []
        for h in range(2):
            m, acc = carry[h]
            kb = k_ref[pl.ds(pl.multiple_of(j * tk, tk), tk), h * HEAD_BLOCK:(h + 1) * HEAD_BLOCK]
            s = _dot_nt(kb, qs[h])
            if masked:
                key = lax.broadcasted_iota(jnp.int32, s.shape, 0)
                qry = lax.broadcasted_iota(jnp.int32, s.shape, 1)
                s = jnp.where(key <= qry, s, -jnp.inf)
            m_new = jnp.maximum(m, jnp.max(s, axis=0, keepdims=True))
            p = jnp.exp(s - m_new).astype(BF16)
            v_ext = jnp.concatenate([vt_ref[j, h * MLA_V:(h + 1) * MLA_V, :], ones_rows], axis=0)
            acc = jnp.exp(m - m_new) * acc + _dot(v_ext, p)
            new.append((m_new, acc))
        return tuple(new)

    init = tuple((jnp.full((1, tq), -jnp.inf, F32), jnp.zeros((MLA_V + 16, tq), F32)) for _ in range(2))
    carry = lax.fori_loop(0, qi, lambda j, c: step(j, c, False), init)
    carry = step(qi, carry, True)
    o_t = jnp.concatenate([acc[:MLA_V] / acc[MLA_V:MLA_V + 1] for _, acc in carry], axis=0)
    out_ref[...] = o_t.T.astype(out_ref.dtype)


def _attention(q, k, vt, *, batch, seq, tq):
    t = q.shape[0]
    nq = seq // tq
    pairs = MLA_HEADS // 2
    return pl.pallas_call(
        functools.partial(_attn_kernel, tq=tq),
        out_shape=jax.ShapeDtypeStruct((t, MLA_WIDTH), BF16),
        grid=(batch, pairs, nq),
        in_specs=[pl.BlockSpec((tq, 2 * HEAD_BLOCK), lambda b, p, i: (b * nq + i, p)),
                  pl.BlockSpec((seq, 2 * HEAD_BLOCK), lambda b, p, i: (b, p)),
                  pl.BlockSpec((None, seq // tq, 2 * MLA_V, tq), lambda b, p, i: (b, 0, p, 0))],
        out_specs=pl.BlockSpec((tq, 2 * MLA_V), lambda b, p, i: (b * nq + i, p)),
        compiler_params=_params("parallel", "parallel", "arbitrary"), name="attn",
    )(q, k, vt)


def _out_proj_kernel(hml_ref, hmla_ref, x_ref, wa_ref, wb_ref, gpost_ref, gpre_ref, x1_ref, m_ref):
    mix = _dot(hml_ref[...], wa_ref[...]) + _dot(hmla_ref[...], wb_ref[...])
    x1 = x_ref[...] + _rms(mix, gpost_ref[...])
    x1_ref[...] = x1
    m_ref[...] = _rms(x1, gpre_ref[...]).astype(m_ref.dtype)


def _out_proj(hml, hmla, x2, wa, wb, gpost, gpre, *, tm):
    t, d = x2.shape
    row = lambda i: (i, 0)
    const = lambda i: (0, 0)
    return pl.pallas_call(
        _out_proj_kernel,
        out_shape=(jax.ShapeDtypeStruct((t, d), F32), jax.ShapeDtypeStruct((t, d), BF16)),
        grid=(t // tm,),
        in_specs=[pl.BlockSpec((tm, ML_WIDTH), row), pl.BlockSpec((tm, MLA_WIDTH), row), pl.BlockSpec((tm, d), row),
                  pl.BlockSpec(wa.shape, const), pl.BlockSpec(wb.shape, const),
                  pl.BlockSpec(gpost.shape, const), pl.BlockSpec(gpre.shape, const)],
        out_specs=(pl.BlockSpec((tm, d), row), pl.BlockSpec((tm, d), row)),
        compiler_params=_params("parallel"), name="out_proj",
    )(hml, hmla, x2, wa, wb, gpost, gpre)


def _ffn_kernel(m_ref, x_ref, wu_ref, wd_ref, g_ref, out_ref, acc_ref):
    j = pl.program_id(1)
    h = jnp.square(jnp.maximum(_dot(m_ref[...], wu_ref[...]), 0.0)).astype(BF16)
    part = _dot(h, wd_ref[...])

    @pl.when(j == 0)
    def _():
        acc_ref[...] = part

    @pl.when(j > 0)
    def _():
        acc_ref[...] += part

    @pl.when(j == pl.num_programs(1) - 1)
    def _():
        out_ref[...] = x_ref[...] + _rms(acc_ref[...], g_ref[...])


def _ffn(m, x1, wu, wd, g, *, tm, tf):
    t, d = x1.shape
    dff = wu.shape[1]
    return pl.pallas_call(
        _ffn_kernel,
        out_shape=jax.ShapeDtypeStruct((t, d), F32),
        grid=(t // tm, dff // tf),
        in_specs=[pl.BlockSpec((tm, d), lambda i, j: (i, 0)), pl.BlockSpec((tm, d), lambda i, j: (i, 0)),
                  pl.BlockSpec((d, tf), lambda i, j: (0, j)), pl.BlockSpec((tf, d), lambda i, j: (j, 0)),
                  pl.BlockSpec(g.shape, lambda i, j: (0, 0))],
        out_specs=pl.BlockSpec((tm, d), lambda i, j: (i, 0)),
        scratch_shapes=[pltpu.VMEM((tm, d), F32)],
        compiler_params=_params("parallel", "arbitrary"), name="ffn",
    )(m, x1, wu, wd, g)


def _layout_in_proj(w_in, b_gates):
    d = w_in.shape[0]
    o0 = QK_COLS
    o1 = o0 + ML_WIDTH
    o2 = o1 + ML_WIDTH
    o3 = o2 + 2 * ML_HEADS
    o4 = o3 + MLA_Q_RANK
    o5 = o4 + MLA_KV_RANK
    gates, k_r = w_in[:, o2:o3], w_in[:, o5:]
    pad = jnp.zeros((d, LANES - GATE_LANE0 - ML_HEADS), w_in.dtype)
    lead = jnp.zeros((d, ROPE_LANE0), w_in.dtype)
    blka = jnp.concatenate([lead, k_r, gates[:, :ML_HEADS], pad], axis=1)
    blkb = jnp.concatenate([lead, -k_r[:, HALF_ROPE:], k_r[:, :HALF_ROPE], gates[:, ML_HEADS:], pad], axis=1)
    win = jnp.concatenate([w_in[:, :o2], w_in[:, o3:o5], blka, blkb], axis=1).astype(BF16)
    zb = jnp.zeros((GATE_LANE0,), b_gates.dtype)
    ze = jnp.zeros((LANES - GATE_LANE0 - ML_HEADS,), b_gates.dtype)
    gbias = jnp.concatenate([zb, b_gates[:ML_HEADS], ze, zb, b_gates[ML_HEADS:], ze]).reshape(1, 2 * LANES)
    return win, gbias


def _layout_mla(w_uq, w_ukv):
    rq, rkv = w_uq.shape[0], w_ukv.shape[0]
    wq3 = w_uq.reshape(rq, MLA_HEADS, MLA_NOPE + MLA_ROPE)
    nope, rope = wq3[:, :, :MLA_NOPE], wq3[:, :, MLA_NOPE:]
    pad = jnp.zeros((rq, MLA_HEADS, HEAD_BLOCK - GATE_LANE0), w_uq.dtype)
    lead = jnp.zeros((rq, MLA_HEADS, ROPE_LANE0), w_uq.dtype)
    wq = jnp.concatenate([nope, rope, pad], axis=2).reshape(rq, MLA_HEADS * HEAD_BLOCK).astype(BF16)
    wqs = jnp.concatenate([lead, -rope[:, :, HALF_ROPE:], rope[:, :, :HALF_ROPE], pad], axis=2)
    wqs = wqs.reshape(rq, MLA_HEADS * HEAD_BLOCK).astype(BF16)
    wkv3 = w_ukv.reshape(rkv, MLA_HEADS, MLA_NOPE + MLA_V)
    kpad = jnp.zeros((rkv, MLA_HEADS, HEAD_BLOCK - MLA_NOPE), w_ukv.dtype)
    wk = jnp.concatenate([wkv3[:, :, :MLA_NOPE], kpad], axis=2).reshape(rkv, MLA_HEADS * HEAD_BLOCK).astype(BF16)
    wvt = wkv3[:, :, MLA_NOPE:].reshape(rkv, MLA_WIDTH).T.astype(BF16)
    return wq, wqs, wk, wvt


def kernel(x, positions, norm_pre_mix, w_in, b_gates, conv_w, conv_b, ml_head_norm, q_norm, w_uq, kv_norm, w_ukv,
           w_out, norm_post_mix, norm_pre_mlp, w_up, w_down, norm_post_mlp):
    batch, seq, d = x.shape
    depth = w_in.shape[0]
    t = batch * seq
    tm_in = min(512, seq)
    tk = 256
    chunk = 256
    tm_out = min(512, t)
    tm_ffn = min(1024, t)
    tf = 1024

    cos_t, sin_t = _rope_tables(positions, tm_in)
    x2 = x.reshape(t, d)
    row = lambda v: v.reshape(1, -1)
    for l in range(depth):
        win, gbias = _layout_in_proj(w_in[l], b_gates[l])
        wq, wqs, wk, wvt = _layout_mla(w_uq[l], w_ukv[l])
        qk, v, o, gb, q, k, vt = _in_proj(
            x2, cos_t, sin_t, row(norm_pre_mix[l]), win, gbias, row(q_norm[l]), wq, wqs, row(kv_norm[l]), wk, wvt,
            batch=batch, seq=seq, tm=tm_in, tk=tk)
        hml = _mlstm(qk, v, o, gb, conv_w[l], row(conv_b[l]), row(ml_head_norm[l]),
                     batch=batch, seq=seq, chunk=chunk)
        hmla = _attention(q, k, vt, batch=batch, seq=seq, tq=tk)
        wo = w_out[l].astype(BF16)
        x1, m = _out_proj(hml, hmla, x2, wo[:ML_WIDTH], wo[ML_WIDTH:], row(norm_post_mix[l]), row(norm_pre_mlp[l]),
                          tm=tm_out)
        x2 = _ffn(m, x1, w_up[l].astype(BF16), w_down[l].astype(BF16), row(norm_post_mlp[l]), tm=tm_ffn, tf=tf)
    return x2.reshape(batch, seq, d)
```

```python
import functools

import jax
import jax.numpy as jnp
from jax import lax
from jax.experimental import pallas as pl
from jax.experimental.pallas import tpu as pltpu

F32 = jnp.float32
BF16 = jnp.bfloat16

EPS = 1e-6
ML_HEADS, ML_QK, ML_V, ML_CONV = 4, 64, 128, 4
MLA_HEADS, MLA_NOPE, MLA_ROPE, MLA_V = 8, 64, 32, 64
MLA_Q_RANK, MLA_KV_RANK = 256, 128
ROPE_THETA = 10000.0
LOG2_E = 1.4426950408889634

ML_WIDTH = ML_HEADS * ML_V
QK_COLS = 2 * ML_HEADS * ML_QK
MLA_WIDTH = MLA_HEADS * MLA_V

LANES = 128
HEAD_BLOCK = LANES
ROPE_LANE0 = MLA_NOPE
GATE_LANE0 = MLA_NOPE + MLA_ROPE
HALF_ROPE = MLA_ROPE // 2
C_QK, C_V, C_O = 0, QK_COLS, QK_COLS + ML_WIDTH
C_CQ = C_O + ML_WIDTH
C_CKV = C_CQ + MLA_Q_RANK
C_BLKA = C_CKV + MLA_KV_RANK
C_BLKB = C_BLKA + LANES
IN_COLS = C_BLKB + LANES

VMEM_LIMIT = 56 * 1024 * 1024


def _params(*sem):
    return pltpu.CompilerParams(dimension_semantics=sem, vmem_limit_bytes=VMEM_LIMIT)


def _rms(x, g):
    return x * lax.rsqrt(jnp.mean(x * x, axis=-1, keepdims=True) + EPS) * g


def _log_sigmoid(x):
    return -(jnp.maximum(-x, 0.0) + jnp.log1p(jnp.exp(-jnp.abs(x))))


def _dot(a, b):
    return jnp.dot(a, b, preferred_element_type=F32)


def _dot_nt(a, b):
    return lax.dot_general(a, b, (((1,), (1,)), ((), ())), preferred_element_type=F32)


def _dot_tn(a, b):
    return lax.dot_general(a, b, (((0,), (0,)), ((), ())), preferred_element_type=F32)


def _rope_table_kernel(pos_ref, inv_ref, cos_ref, sin_ref):
    ang = pos_ref[...].astype(F32) * inv_ref[...]
    lane = lax.broadcasted_iota(jnp.int32, ang.shape, 1)
    rope = (lane >= ROPE_LANE0) & (lane < GATE_LANE0)
    cos_ref[...] = jnp.where(lane < ROPE_LANE0, 1.0, jnp.where(rope, jnp.cos(ang), 0.0))
    sin_ref[...] = jnp.where(rope, jnp.sin(ang), 0.0)


def _rope_tables(positions, tm):
    t = positions.size
    inv = 1.0 / (ROPE_THETA ** (jnp.arange(0, MLA_ROPE, 2, dtype=F32) / MLA_ROPE))
    inv_row = jnp.zeros((1, LANES), F32)
    inv_row = inv_row.at[0, ROPE_LANE0:ROPE_LANE0 + HALF_ROPE].set(inv)
    inv_row = inv_row.at[0, ROPE_LANE0 + HALF_ROPE:GATE_LANE0].set(inv)
    return pl.pallas_call(
        _rope_table_kernel,
        out_shape=(jax.ShapeDtypeStruct((t, LANES), F32),) * 2,
        grid=(t // tm,),
        in_specs=[pl.BlockSpec((tm, 1), lambda i: (i, 0)),
                  pl.BlockSpec((1, LANES), lambda i: (0, 0))],
        out_specs=(pl.BlockSpec((tm, LANES), lambda i: (i, 0)),) * 2,
        compiler_params=_params("parallel"),
        name="rope_tables",
    )(positions.reshape(t, 1), inv_row)


def _in_proj_kernel(x_ref, cos_ref, sin_ref, g_ref, win_ref, gbias_ref, qn_ref, wq_ref, wqs_ref,
                    kvn_ref, wk_ref, wvt_ref,
                    qk_ref, v_ref, o_ref, gb_ref, q_ref, k_ref, vt_ref):
    a = _rms(x_ref[...], g_ref[...]).astype(BF16)
    proj = _dot(a, win_ref[...])
    qk_ref[...] = proj[:, C_QK:C_V]
    v_ref[...] = proj[:, C_V:C_O].astype(BF16)
    o_ref[...] = proj[:, C_O:C_CQ]
    blka = proj[:, C_BLKA:C_BLKB]
    blkb = proj[:, C_BLKB:IN_COLS]
    gb_ref[...] = proj[:, C_BLKA:IN_COLS] + gbias_ref[...]

    cos_t = cos_ref[...]
    sin_t = sin_ref[...]
    scale = (MLA_NOPE + MLA_ROPE) ** -0.5 * LOG2_E

    qn = _rms(proj[:, C_CQ:C_CKV], qn_ref[...]).astype(BF16)
    qf = _dot(qn, wq_ref[...])
    qs = _dot(qn, wqs_ref[...])
    ckvn = _rms(proj[:, C_CKV:C_BLKA], kvn_ref[...]).astype(BF16)
    kf = _dot(ckvn, wk_ref[...])
    lane = lax.broadcasted_iota(jnp.int32, blka.shape, 1)
    rope = (lane >= ROPE_LANE0) & (lane < GATE_LANE0)
    k_rope = jnp.where(rope, blka * cos_t + blkb * sin_t, 0.0)
    for h in range(MLA_HEADS):
        blk = slice(h * HEAD_BLOCK, (h + 1) * HEAD_BLOCK)
        q_ref[:, blk] = ((qf[:, blk] * cos_t + qs[:, blk] * sin_t) * scale).astype(BF16)
        k_ref[:, blk] = (kf[:, blk] + k_rope).astype(BF16)

    vt = _dot_nt(wvt_ref[...], ckvn).astype(BF16)
    if len(vt_ref.shape) == 2:
        vt_ref[...] = vt
    else:
        tk = vt_ref.shape[-1]
        for j in range(vt_ref.shape[0]):
            vt_ref[j] = vt[:, j * tk:(j + 1) * tk]


def _in_proj(x2, cos_t, sin_t, g, win, gbias, qn, wq, wqs, kvn, wk, wvt, *, batch, seq, tm, tk):
    t, d = x2.shape
    nt = t // tm
    tiles_per_batch = seq // tm
    if tk >= tm:
        per_kv = tk // tm
        vt_spec = pl.BlockSpec((None, None, MLA_WIDTH, tm),
                               lambda i: (i // tiles_per_batch, (i % tiles_per_batch) // per_kv, 0, i % per_kv))
    else:
        vt_spec = pl.BlockSpec((None, tm // tk, MLA_WIDTH, tk),
                               lambda i: (i // tiles_per_batch, i % tiles_per_batch, 0, 0))
    row = lambda i: (i, 0)
    const = lambda i: (0, 0)
    full = lambda arr: pl.BlockSpec(arr.shape, const)
    out_shape = (
        jax.ShapeDtypeStruct((t, QK_COLS), F32),
        jax.ShapeDtypeStruct((t, ML_WIDTH), BF16),
        jax.ShapeDtypeStruct((t, ML_WIDTH), F32),
        jax.ShapeDtypeStruct((t, 2 * LANES), F32),
        jax.ShapeDtypeStruct((t, MLA_HEADS * HEAD_BLOCK), BF16),
        jax.ShapeDtypeStruct((t, MLA_HEADS * HEAD_BLOCK), BF16),
        jax.ShapeDtypeStruct((batch, seq // tk, MLA_WIDTH, tk), BF16),
    )
    out_specs = (
        pl.BlockSpec((tm, QK_COLS), row),
        pl.BlockSpec((tm, ML_WIDTH), row),
        pl.BlockSpec((tm, ML_WIDTH), row),
        pl.BlockSpec((tm, 2 * LANES), row),
        pl.BlockSpec((tm, MLA_HEADS * HEAD_BLOCK), row),
        pl.BlockSpec((tm, MLA_HEADS * HEAD_BLOCK), row),
        vt_spec,
    )
    in_specs = [pl.BlockSpec((tm, d), row), pl.BlockSpec((tm, LANES), row), pl.BlockSpec((tm, LANES), row),
                full(g), full(win), full(gbias), full(qn), full(wq), full(wqs), full(kvn), full(wk), full(wvt)]
    return pl.pallas_call(
        _in_proj_kernel,
        out_shape=out_shape, grid=(nt,), in_specs=in_specs, out_specs=out_specs,
        compiler_params=_params("parallel"), name="in_proj",
    )(x2, cos_t, sin_t, g, win, gbias, qn, wq, wqs, kvn, wk, wvt)


def _scan_rows(x, op, identity):
    n = x.shape[0]
    row = lax.broadcasted_iota(jnp.int32, x.shape, 0)
    step = 1
    while step < n:
        x = op(x, jnp.where(row >= step, pltpu.roll(x, step, axis=0), identity))
        step *= 2
    return x


def _mlstm_kernel(qk_ref, v_ref, o_ref, gb_ref, cw_ref, cb_ref, hn_ref, out_ref, xbuf, c_ref, m_ref, *, chunk):
    L = chunk
    halo = 8

    @pl.when(pl.program_id(1) == 0)
    def _():
        xbuf[0:halo, :] = jnp.zeros((halo, QK_COLS), F32)
        c_ref[...] = jnp.zeros_like(c_ref)
        m_ref[...] = jnp.zeros_like(m_ref)

    xbuf[halo:halo + L, :] = qk_ref[...]
    cw = cw_ref[...]
    y = cb_ref[...]
    for tap in range(ML_CONV):
        back = ML_CONV - 1 - tap
        y = y + xbuf[halo - back:halo - back + L, :] * cw[tap:tap + 1, :]
    xbuf[0:halo, :] = xbuf[L:L + halo, :]
    qk = y * jax.nn.sigmoid(y)
    q_all = qk[:, :ML_HEADS * ML_QK]
    k_all = qk[:, ML_HEADS * ML_QK:] * (ML_QK ** -0.5)

    gb = gb_ref[...]
    ig = gb[:, :LANES]
    b = _scan_rows(_log_sigmoid(gb[:, LANES:]), jnp.add, 0.0)
    g = ig - b
    m_prev = m_ref[...]
    big_m = jnp.maximum(_scan_rows(g, jnp.maximum, -jnp.inf), m_prev)
    inter_w = jnp.exp(m_prev - big_m)
    floor = jnp.exp(-(b + big_m))
    m_end = big_m[L - 1:L, :]
    w_state = jnp.exp(g - m_end)
    m_ref[...] = b[L - 1:L, :] + m_end
    g_t = g.T

    row = lax.broadcasted_iota(jnp.int32, (L, L), 0)
    col = lax.broadcasted_iota(jnp.int32, (L, L), 1)
    causal = row >= col
    lane = lax.broadcasted_iota(jnp.int32, (L, LANES), 1)
    ones_blk = jnp.where(lane == 0, 1.0, 0.0).astype(BF16)

    for h in range(ML_HEADS):
        pair, lo = h // 2, (h % 2) * ML_QK
        gl = GATE_LANE0 + h
        blk = slice(pair * LANES, (pair + 1) * LANES)
        mine = (lane >= lo) & (lane < lo + ML_QK)
        qm = jnp.where(mine, q_all[:, blk], 0.0).astype(BF16)
        k_blk = k_all[:, blk]
        s = _dot_nt(qm, k_blk.astype(BF16))
        dw = jnp.where(causal, jnp.exp(g_t[gl:gl + 1, :] - big_m[:, gl:gl + 1]), 0.0)
        v_ext = jnp.concatenate([v_ref[:, h * ML_V:(h + 1) * ML_V], ones_blk], axis=1)
        c_pair = c_ref[pair]
        num = _dot((s * dw).astype(BF16), v_ext) + inter_w[:, gl:gl + 1] * _dot(qm, c_pair.astype(BF16))
        den = num[:, ML_V:ML_V + 1]
        hh = num[:, :ML_V] / jnp.maximum(jnp.abs(den), floor[:, gl:gl + 1])
        hs = slice(h * ML_V, (h + 1) * ML_V)
        hh = _rms(hh, hn_ref[:, hs]) * jax.nn.sigmoid(o_ref[:, hs])
        out_ref[:, hs] = hh.astype(out_ref.dtype)

        kw = (k_blk * w_state[:, gl:gl + 1]).astype(BF16)
        upd = _dot_tn(kw, v_ext)
        rows = slice(lo, lo + ML_QK)
        c_ref[pair, rows, :] = inter_w[L - 1:L, gl:gl + 1] * c_pair[rows, :] + upd[rows, :]


def _mlstm(qk, v, o, gb, conv_w, conv_b, head_norm, *, batch, seq, chunk):
    t = qk.shape[0]
    nc = seq // chunk
    row = lambda b, c: (b * nc + c, 0)
    const = lambda b, c: (0, 0)
    return pl.pallas_call(
        functools.partial(_mlstm_kernel, chunk=chunk),
        out_shape=jax.ShapeDtypeStruct((t, ML_WIDTH), BF16),
        grid=(batch, nc),
        in_specs=[pl.BlockSpec((chunk, QK_COLS), row), pl.BlockSpec((chunk, ML_WIDTH), row),
                  pl.BlockSpec((chunk, ML_WIDTH), row), pl.BlockSpec((chunk, 2 * LANES), row),
                  pl.BlockSpec(conv_w.shape, const), pl.BlockSpec(conv_b.shape, const),
                  pl.BlockSpec(head_norm.shape, const)],
        out_specs=pl.BlockSpec((chunk, ML_WIDTH), row),
        scratch_shapes=[pltpu.VMEM((chunk + 8, QK_COLS), F32),
                        pltpu.VMEM((ML_HEADS // 2, LANES, 2 * LANES), F32),
                        pltpu.VMEM((1, LANES), F32)],
        compiler_params=_params("parallel", "arbitrary"), name="mlstm",
    )(qk, v, o, gb, conv_w, conv_b, head_norm)


def _attn_kernel(q_ref, k_ref, vt_ref, out_ref, s_sc, smax_sc, m_sc, acc_sc, *, tb):
    last = pl.program_id(2)
    heads = range(2)
    ones_rows = jnp.where(lax.broadcasted_iota(jnp.int32, (16, tb), 0) == 0, 1.0, 0.0).astype(BF16)

    def produce(j, slot, diagonal):
        for h in heads:
            kb = k_ref[pl.ds(pl.multiple_of(j * tb, tb), tb), h * HEAD_BLOCK:(h + 1) * HEAD_BLOCK]
            s = _dot_nt(kb, q_ref[:, h * HEAD_BLOCK:(h + 1) * HEAD_BLOCK])
            if diagonal:
                key = lax.broadcasted_iota(jnp.int32, s.shape, 0)
                qry = lax.broadcasted_iota(jnp.int32, s.shape, 1)
                s = jnp.where(key <= qry, s, -jnp.inf)
            s_sc[slot, h] = s
            smax_sc[slot, h] = jnp.max(s, axis=0, keepdims=True)

    def consume(j, slot):
        for h in heads:
            m = m_sc[h]
            m_new = jnp.maximum(m, smax_sc[slot, h])
            p = jnp.exp2(s_sc[slot, h] - m_new).astype(BF16)
            v_ext = jnp.concatenate([vt_ref[j, h * MLA_V:(h + 1) * MLA_V, :], ones_rows], axis=0)
            acc_sc[h] = jnp.exp2(m - m_new) * acc_sc[h] + _dot(v_ext, p)
            m_sc[h] = m_new

    def stage(j, slot, next_is_diagonal):
        consume(j, slot)
        produce(j + 1, 1 - slot, next_is_diagonal)

    def finish(slot):
        consume(last, slot)
        o_t = jnp.concatenate([acc_sc[h, :MLA_V, :] / acc_sc[h, MLA_V:MLA_V + 1, :] for h in heads], axis=0)
        out_ref[...] = o_t.T.astype(out_ref.dtype)

    m_sc[...] = jnp.full(m_sc.shape, -jnp.inf, F32)
    acc_sc[...] = jnp.zeros(acc_sc.shape, F32)

    @pl.when(last == 0)
    def _():
        produce(0, 0, True)
        finish(0)

    @pl.when(last > 0)
    def _():
        produce(0, 0, False)

        def two_stages(t, carry):
            stage(2 * t, 0, False)
            stage(2 * t + 1, 1, False)
            return carry

        lax.fori_loop(0, (last - 1) // 2, two_stages, 0)

        @pl.when(last % 2 == 1)
        def _():
            stage(last - 1, 0, True)
            finish(1)

        @pl.when(last % 2 == 0)
        def _():
            stage(last - 2, 0, False)
            stage(last - 1, 1, True)
            finish(0)


def _attention(q, k, vt, *, batch, seq, tb):
    t = q.shape[0]
    nb = seq // tb
    pairs = MLA_HEADS // 2
    return pl.pallas_call(
        functools.partial(_attn_kernel, tb=tb),
        out_shape=jax.ShapeDtypeStruct((t, MLA_WIDTH), BF16),
        grid=(batch, pairs, nb),
        in_specs=[pl.BlockSpec((tb, 2 * HEAD_BLOCK), lambda b, p, i: (b * nb + i, p)),
                  pl.BlockSpec((seq, 2 * HEAD_BLOCK), lambda b, p, i: (b, p)),
                  pl.BlockSpec((None, nb, 2 * MLA_V, tb), lambda b, p, i: (b, 0, p, 0))],
        out_specs=pl.BlockSpec((tb, 2 * MLA_V), lambda b, p, i: (b * nb + i, p)),
        scratch_shapes=[pltpu.VMEM((2, 2, tb, tb), F32), pltpu.VMEM((2, 2, 1, tb), F32),
                        pltpu.VMEM((2, 1, tb), F32), pltpu.VMEM((2, MLA_V + 16, tb), F32)],
        compiler_params=_params("parallel", "parallel", "arbitrary"), name="attn",
    )(q, k, vt)


def _out_proj_kernel(hml_ref, hmla_ref, x_ref, wa_ref, wb_ref, gpost_ref, gpre_ref, x1_ref, m_ref):
    mix = _dot(hml_ref[...], wa_ref[...]) + _dot(hmla_ref[...], wb_ref[...])
    x1 = x_ref[...] + _rms(mix, gpost_ref[...])
    x1_ref[...] = x1
    m_ref[...] = _rms(x1, gpre_ref[...]).astype(m_ref.dtype)


def _out_proj(hml, hmla, x2, wa, wb, gpost, gpre, *, tm):
    t, d = x2.shape
    row = lambda i: (i, 0)
    const = lambda i: (0, 0)
    return pl.pallas_call(
        _out_proj_kernel,
        out_shape=(jax.ShapeDtypeStruct((t, d), F32), jax.ShapeDtypeStruct((t, d), BF16)),
        grid=(t // tm,),
        in_specs=[pl.BlockSpec((tm, ML_WIDTH), row), pl.BlockSpec((tm, MLA_WIDTH), row), pl.BlockSpec((tm, d), row),
                  pl.BlockSpec(wa.shape, const), pl.BlockSpec(wb.shape, const),
                  pl.BlockSpec(gpost.shape, const), pl.BlockSpec(gpre.shape, const)],
        out_specs=(pl.BlockSpec((tm, d), row), pl.BlockSpec((tm, d), row)),
        compiler_params=_params("parallel"), name="out_proj",
    )(hml, hmla, x2, wa, wb, gpost, gpre)


def _ffn_kernel(m_ref, x_ref, wu_ref, wd_ref, g_ref, out_ref, acc_ref):
    j = pl.program_id(1)
    h = jnp.square(jnp.maximum(_dot(m_ref[...], wu_ref[...]), 0.0)).astype(BF16)
    part = _dot(h, wd_ref[...])

    @pl.when(j == 0)
    def _():
        acc_ref[...] = part

    @pl.when(j > 0)
    def _():
        acc_ref[...] += part

    @pl.when(j == pl.num_programs(1) - 1)
    def _():
        out_ref[...] = x_ref[...] + _rms(acc_ref[...], g_ref[...])


def _ffn(m, x1, wu, wd, g, *, tm, tf):
    t, d = x1.shape
    dff = wu.shape[1]
    return pl.pallas_call(
        _ffn_kernel,
        out_shape=jax.ShapeDtypeStruct((t, d), F32),
        grid=(t // tm, dff // tf),
        in_specs=[pl.BlockSpec((tm, d), lambda i, j: (i, 0)), pl.BlockSpec((tm, d), lambda i, j: (i, 0)),
                  pl.BlockSpec((d, tf), lambda i, j: (0, j)), pl.BlockSpec((tf, d), lambda i, j: (j, 0)),
                  pl.BlockSpec(g.shape, lambda i, j: (0, 0))],
        out_specs=pl.BlockSpec((tm, d), lambda i, j: (i, 0)),
        scratch_shapes=[pltpu.VMEM((tm, d), F32)],
        compiler_params=_params("parallel", "arbitrary"), name="ffn",
    )(m, x1, wu, wd, g)


def _layout_in_proj(w_in, b_gates):
    d = w_in.shape[0]
    o0 = QK_COLS
    o1 = o0 + ML_WIDTH
    o2 = o1 + ML_WIDTH
    o3 = o2 + 2 * ML_HEADS
    o4 = o3 + MLA_Q_RANK
    o5 = o4 + MLA_KV_RANK
    gates, k_r = w_in[:, o2:o3], w_in[:, o5:]
    pad = jnp.zeros((d, LANES - GATE_LANE0 - ML_HEADS), w_in.dtype)
    lead = jnp.zeros((d, ROPE_LANE0), w_in.dtype)
    blka = jnp.concatenate([lead, k_r, gates[:, :ML_HEADS], pad], axis=1)
    blkb = jnp.concatenate([lead, -k_r[:, HALF_ROPE:], k_r[:, :HALF_ROPE], gates[:, ML_HEADS:], pad], axis=1)
    win = jnp.concatenate([w_in[:, :o2], w_in[:, o3:o5], blka, blkb], axis=1).astype(BF16)
    zb = jnp.zeros((GATE_LANE0,), b_gates.dtype)
    ze = jnp.zeros((LANES - GATE_LANE0 - ML_HEADS,), b_gates.dtype)
    gbias = jnp.concatenate([zb, b_gates[:ML_HEADS], ze, zb, b_gates[ML_HEADS:], ze]).reshape(1, 2 * LANES)
    return win, gbias


def _layout_mla(w_uq, w_ukv):
    rq, rkv = w_uq.shape[0], w_ukv.shape[0]
    wq3 = w_uq.reshape(rq, MLA_HEADS, MLA_NOPE + MLA_ROPE)
    nope, rope = wq3[:, :, :MLA_NOPE], wq3[:, :, MLA_NOPE:]
    pad = jnp.zeros((rq, MLA_HEADS, HEAD_BLOCK - GATE_LANE0), w_uq.dtype)
    lead = jnp.zeros((rq, MLA_HEADS, ROPE_LANE0), w_uq.dtype)
    wq = jnp.concatenate([nope, rope, pad], axis=2).reshape(rq, MLA_HEADS * HEAD_BLOCK).astype(BF16)
    wqs = jnp.concatenate([lead, -rope[:, :, HALF_ROPE:], rope[:, :, :HALF_ROPE], pad], axis=2)
    wqs = wqs.reshape(rq, MLA_HEADS * HEAD_BLOCK).astype(BF16)
    wkv3 = w_ukv.reshape(rkv, MLA_HEADS, MLA_NOPE + MLA_V)
    kpad = jnp.zeros((rkv, MLA_HEADS, HEAD_BLOCK - MLA_NOPE), w_ukv.dtype)
    wk = jnp.concatenate([wkv3[:, :, :MLA_NOPE], kpad], axis=2).reshape(rkv, MLA_HEADS * HEAD_BLOCK).astype(BF16)
    wvt = wkv3[:, :, MLA_NOPE:].reshape(rkv, MLA_WIDTH).T.astype(BF16)
    return wq, wqs, wk, wvt


def kernel(x, positions, norm_pre_mix, w_in, b_gates, conv_w, conv_b, ml_head_norm, q_norm, w_uq, kv_norm, w_ukv,
           w_out, norm_post_mix, norm_pre_mlp, w_up, w_down, norm_post_mlp):
    batch, seq, d = x.shape
    depth = w_in.shape[0]
    t = batch * seq
    tm_in = min(512, seq)
    tb = min(512, seq)
    chunk = 256
    tm_out = min(512, t)
    tm_ffn = min(1024, t)
    tf = 1024

    cos_t, sin_t = _rope_tables(positions, tm_in)
    x2 = x.reshape(t, d)
    row = lambda v: v.reshape(1, -1)
    for l in range(depth):
        win, gbias = _layout_in_proj(w_in[l], b_gates[l])
        wq, wqs, wk, wvt = _layout_mla(w_uq[l], w_ukv[l])
        qk, v, o, gb, q, k, vt = _in_proj(
            x2, cos_t, sin_t, row(norm_pre_mix[l]), win, gbias, row(q_norm[l]), wq, wqs, row(kv_norm[l]), wk, wvt,
            batch=batch, seq=seq, tm=tm_in, tk=tb)
        hml = _mlstm(qk, v, o, gb, conv_w[l], row(conv_b[l]), row(ml_head_norm[l]),
                     batch=batch, seq=seq, chunk=chunk)
        hmla = _attention(q, k, vt, batch=batch, seq=seq, tb=tb)
        wo = w_out[l].astype(BF16)
        x1, m = _out_proj(hml, hmla, x2, wo[:ML_WIDTH], wo[ML_WIDTH:], row(norm_post_mix[l]), row(norm_pre_mlp[l]),
                          tm=tm_out)
        x2 = _ffn(m, x1, w_up[l].astype(BF16), w_down[l].astype(BF16), row(norm_post_mlp[l]), tm=tm_ffn, tf=tf)
    return x2.reshape(batch, seq, d)
```

```python
import functools

import jax
import jax.numpy as jnp
from jax import lax
from jax.experimental import pallas as pl
from jax.experimental.pallas import tpu as pltpu

F32 = jnp.float32
BF16 = jnp.bfloat16

EPS = 1e-6
ML_HEADS, ML_QK, ML_V, ML_CONV = 4, 64, 128, 4
MLA_HEADS, MLA_NOPE, MLA_ROPE, MLA_V = 8, 64, 32, 64
MLA_Q_RANK, MLA_KV_RANK = 256, 128
ROPE_THETA = 10000.0
LOG2_E = 1.4426950408889634

ML_WIDTH = ML_HEADS * ML_V
ML_QK_WIDTH = ML_HEADS * ML_QK
QK_COLS = 2 * ML_QK_WIDTH
MLA_WIDTH = MLA_HEADS * MLA_V

LANES = 128
SUBLANES = 8
HEAD_BLOCK = LANES
ROPE_LANE0 = MLA_NOPE
ROPE_LANE1 = MLA_NOPE + MLA_ROPE
HALF_ROPE = MLA_ROPE // 2
C_QK, C_V, C_O = 0, QK_COLS, QK_COLS + ML_WIDTH
C_CQ = C_O + ML_WIDTH
C_CKV = C_CQ + MLA_Q_RANK
C_BLKA = C_CKV + MLA_KV_RANK
C_BLKB = C_BLKA + LANES
IN_COLS = C_BLKB + LANES
CONV_HALO = SUBLANES

VMEM_LIMIT = 56 * 1024 * 1024


def _params(*sem):
    return pltpu.CompilerParams(dimension_semantics=sem, vmem_limit_bytes=VMEM_LIMIT)


def _rms(x, g):
    return x * lax.rsqrt(jnp.mean(x * x, axis=-1, keepdims=True) + EPS) * g


def _log_sigmoid(x):
    return -(jnp.maximum(-x, 0.0) + jnp.log1p(jnp.exp(-jnp.abs(x))))


def _dot(a, b):
    return jnp.dot(a, b, preferred_element_type=F32)


def _dot_nt(a, b):
    return lax.dot_general(a, b, (((1,), (1,)), ((), ())), preferred_element_type=F32)


def _dot_tn(a, b):
    return lax.dot_general(a, b, (((0,), (0,)), ((), ())), preferred_element_type=F32)


def _rope_table_kernel(pos_ref, inv_ref, cos_ref, sin_ref):
    ang = pos_ref[...].astype(F32) * inv_ref[...]
    lane = lax.broadcasted_iota(jnp.int32, ang.shape, 1)
    rope = (lane >= ROPE_LANE0) & (lane < ROPE_LANE1)
    cos_ref[...] = jnp.where(lane < ROPE_LANE0, 1.0, jnp.where(rope, jnp.cos(ang), 0.0))
    sin_ref[...] = jnp.where(rope, jnp.sin(ang), 0.0)


def _rope_tables(positions, tm):
    t = positions.size
    inv = 1.0 / (ROPE_THETA ** (jnp.arange(0, MLA_ROPE, 2, dtype=F32) / MLA_ROPE))
    inv_row = jnp.zeros((1, LANES), F32)
    inv_row = inv_row.at[0, ROPE_LANE0:ROPE_LANE0 + HALF_ROPE].set(inv)
    inv_row = inv_row.at[0, ROPE_LANE0 + HALF_ROPE:ROPE_LANE1].set(inv)
    return pl.pallas_call(
        _rope_table_kernel,
        out_shape=(jax.ShapeDtypeStruct((t, LANES), F32),) * 2,
        grid=(t // tm,),
        in_specs=[pl.BlockSpec((tm, 1), lambda i: (i, 0)),
                  pl.BlockSpec((1, LANES), lambda i: (0, 0))],
        out_specs=(pl.BlockSpec((tm, LANES), lambda i: (i, 0)),) * 2,
        compiler_params=_params("parallel"),
        name="rope_tables",
    )(positions.reshape(t, 1), inv_row)


def _scan_lanes(x, op, identity, segment):
    pos = lax.broadcasted_iota(jnp.int32, x.shape, 1) % segment
    step = 1
    while step < segment:
        x = op(x, jnp.where(pos >= step, pltpu.roll(x, step, axis=1), identity))
        step *= 2
    return x


def _in_proj_kernel(x_ref, cos_ref, sin_ref, g_ref, win_ref, wg_ref, gbias_ref, cw_ref, cb_ref, qn_ref, wq_ref,
                    wqs_ref, kvn_ref, wk_ref, wvt_ref,
                    qml_ref, kml_ref, v_ref, o_ref, gs_ref, q_ref, k_ref, vt_ref, xbuf,
                    *, tiles_per_batch, chunk, sub):
    tm = x_ref.shape[0]
    scale = (MLA_NOPE + MLA_ROPE) ** -0.5 * LOG2_E

    @pl.when(pl.program_id(0) % tiles_per_batch == 0)
    def _():
        xbuf[0:CONV_HALO, :] = jnp.zeros((CONV_HALO, QK_COLS), F32)

    blocks = [slice(r, r + sub) for r in range(0, tm, sub)]
    acts = [_rms(x_ref[rows, :], g_ref[...]).astype(BF16) for rows in blocks]
    projs = [_dot(a, win_ref[...]) for a in acts]

    for rows, a, proj in zip(blocks, acts, projs):
        v_ref[rows, :] = proj[:, C_V:C_O].astype(BF16)
        o_ref[rows, :] = proj[:, C_O:C_CQ].astype(BF16)

        top = CONV_HALO + rows.start
        xbuf[top:top + sub, :] = proj[:, C_QK:C_V]
        cw = cw_ref[...]
        y = cb_ref[...]
        for tap in range(ML_CONV):
            start = top - (ML_CONV - 1 - tap)
            y = y + xbuf[start:start + sub, :] * cw[tap:tap + 1, :]
        qk = y * jax.nn.sigmoid(y)
        qml_ref[rows, :] = qk[:, :ML_QK_WIDTH].astype(BF16)
        kml_ref[rows, :] = (qk[:, ML_QK_WIDTH:] * (ML_QK ** -0.5)).astype(BF16)

        gates = _dot_nt(wg_ref[...], a) + gbias_ref[...]
        b = _scan_lanes(_log_sigmoid(gates[SUBLANES:]), jnp.add, 0.0, chunk)
        g = gates[:SUBLANES] - b
        gs_ref[:, rows] = jnp.concatenate([b, g, _scan_lanes(g, jnp.maximum, -jnp.inf, chunk)], axis=0)

        cos_t = cos_ref[rows, :]
        sin_t = sin_ref[rows, :]
        qn = _rms(proj[:, C_CQ:C_CKV], qn_ref[...]).astype(BF16)
        qf = _dot(qn, wq_ref[...])
        qs = _dot(qn, wqs_ref[...])
        ckvn = _rms(proj[:, C_CKV:C_BLKA], kvn_ref[...]).astype(BF16)
        kf = _dot(ckvn, wk_ref[...])
        k_rope = proj[:, C_BLKA:C_BLKB] * cos_t + proj[:, C_BLKB:IN_COLS] * sin_t
        for h in range(MLA_HEADS):
            blk = slice(h * HEAD_BLOCK, (h + 1) * HEAD_BLOCK)
            q_ref[rows, blk] = ((qf[:, blk] * cos_t + qs[:, blk] * sin_t) * scale).astype(BF16)
            k_ref[rows, blk] = (kf[:, blk] + k_rope).astype(BF16)

        vt_ref[:, rows] = _dot_nt(wvt_ref[...], ckvn).astype(BF16)

    xbuf[0:CONV_HALO, :] = xbuf[tm:tm + CONV_HALO, :]


def _in_proj(x2, cos_t, sin_t, g, win, wg, gbias, conv_w, conv_b, qn, wq, wqs, kvn, wk, wvt,
             *, batch, seq, tm, tk, chunk):
    t, d = x2.shape
    nt = t // tm
    tiles_per_batch = seq // tm
    per_kv = tk // tm
    vt_spec = pl.BlockSpec((None, None, MLA_WIDTH, tm),
                           lambda i: (i // tiles_per_batch, (i % tiles_per_batch) // per_kv, 0, i % per_kv))
    row = lambda i: (i, 0)
    col = lambda i: (0, i)
    const = lambda i: (0, 0)
    full = lambda arr: pl.BlockSpec(arr.shape, const)
    out_shape = (
        jax.ShapeDtypeStruct((t, ML_QK_WIDTH), BF16),
        jax.ShapeDtypeStruct((t, ML_QK_WIDTH), BF16),
        jax.ShapeDtypeStruct((t, ML_WIDTH), BF16),
        jax.ShapeDtypeStruct((t, ML_WIDTH), BF16),
        jax.ShapeDtypeStruct((3 * SUBLANES, t), F32),
        jax.ShapeDtypeStruct((t, MLA_HEADS * HEAD_BLOCK), BF16),
        jax.ShapeDtypeStruct((t, MLA_HEADS * HEAD_BLOCK), BF16),
        jax.ShapeDtypeStruct((batch, seq // tk, MLA_WIDTH, tk), BF16),
    )
    out_specs = (
        pl.BlockSpec((tm, ML_QK_WIDTH), row),
        pl.BlockSpec((tm, ML_QK_WIDTH), row),
        pl.BlockSpec((tm, ML_WIDTH), row),
        pl.BlockSpec((tm, ML_WIDTH), row),
        pl.BlockSpec((3 * SUBLANES, tm), col),
        pl.BlockSpec((tm, MLA_HEADS * HEAD_BLOCK), row),
        pl.BlockSpec((tm, MLA_HEADS * HEAD_BLOCK), row),
        vt_spec,
    )
    in_specs = [pl.BlockSpec((tm, d), row), pl.BlockSpec((tm, LANES), row), pl.BlockSpec((tm, LANES), row),
                full(g), full(win), full(wg), full(gbias), full(conv_w), full(conv_b),
                full(qn), full(wq), full(wqs), full(kvn), full(wk), full(wvt)]
    return pl.pallas_call(
        functools.partial(_in_proj_kernel, tiles_per_batch=tiles_per_batch, chunk=chunk, sub=min(tm, 256)),
        out_shape=out_shape, grid=(nt,), in_specs=in_specs, out_specs=out_specs,
        scratch_shapes=[pltpu.VMEM((tm + CONV_HALO, QK_COLS), F32)],
        compiler_params=_params("arbitrary"), name="in_proj",
    )(x2, cos_t, sin_t, g, win, wg, gbias, conv_w, conv_b, qn, wq, wqs, kvn, wk, wvt)


def _mlstm_kernel(q_ref, k_ref, v_ref, gs_ref, out_ref, c_ref, m_ref, *, chunk):
    L = chunk

    @pl.when(pl.program_id(1) == 0)
    def _():
        c_ref[...] = jnp.zeros_like(c_ref)
        m_ref[...] = jnp.zeros_like(m_ref)

    b, g, big_g = gs_ref[0:SUBLANES, :], gs_ref[SUBLANES:2 * SUBLANES, :], gs_ref[2 * SUBLANES:, :]
    m_prev = jnp.concatenate([m_ref[...]] * (L // LANES), axis=1)
    big_m = jnp.maximum(big_g, m_prev)
    inter_w = jnp.exp(m_prev - big_m)
    floor = jnp.exp(-(b + big_m))
    m_end = big_m[:, L - 1:L]
    w_state = jnp.exp(g - m_end)
    m_ref[...] = jnp.broadcast_to(b[:, L - 1:L] + m_end, m_ref.shape)
    stats = [big_m, inter_w, floor, w_state]
    pad = jnp.zeros((LANES - len(stats) * SUBLANES, L), F32)
    cols = jnp.concatenate(stats + [pad], axis=0).T

    row = lax.broadcasted_iota(jnp.int32, (L, L), 0)
    col = lax.broadcasted_iota(jnp.int32, (L, L), 1)
    causal = row >= col
    lane = lax.broadcasted_iota(jnp.int32, (L, LANES), 1)
    ones_blk = jnp.where(lane == 0, 1.0, 0.0).astype(BF16)

    for h in range(ML_HEADS):
        pair, lo = h // 2, (h % 2) * ML_QK
        blk = slice(pair * LANES, (pair + 1) * LANES)
        mine = (lane >= lo) & (lane < lo + ML_QK)
        qm = jnp.where(mine, q_ref[:, blk], jnp.zeros((), BF16))
        k_blk = k_ref[:, blk]
        s = _dot_nt(qm, k_blk)
        dw = jnp.where(causal, jnp.exp(g[h:h + 1, :] - cols[:, h:h + 1]), 0.0)
        v_ext = jnp.concatenate([v_ref[:, h * ML_V:(h + 1) * ML_V], ones_blk], axis=1)
        c_pair = c_ref[pair]
        iw = cols[:, SUBLANES + h:SUBLANES + h + 1]
        num = _dot((s * dw).astype(BF16), v_ext) + iw * _dot(qm, c_pair.astype(BF16))
        den = num[:, ML_V:ML_V + 1]
        hh = num[:, :ML_V] / jnp.maximum(jnp.abs(den), cols[:, 2 * SUBLANES + h:2 * SUBLANES + h + 1])
        out_ref[:, h * ML_V:(h + 1) * ML_V] = hh.astype(out_ref.dtype)

        kw = (k_blk.astype(F32) * cols[:, 3 * SUBLANES + h:3 * SUBLANES + h + 1]).astype(BF16)
        upd = _dot_tn(kw, v_ext)
        rows = slice(lo, lo + ML_QK)
        c_ref[pair, rows, :] = inter_w[h:h + 1, L - 1:L] * c_pair[rows, :] + upd[rows, :]


def _mlstm(q, k, v, gs, *, batch, seq, chunk):
    t = q.shape[0]
    nc = seq // chunk
    row = lambda b, c: (b * nc + c, 0)
    return pl.pallas_call(
        functools.partial(_mlstm_kernel, chunk=chunk),
        out_shape=jax.ShapeDtypeStruct((t, ML_WIDTH), BF16),
        grid=(batch, nc),
        in_specs=[pl.BlockSpec((chunk, ML_QK_WIDTH), row), pl.BlockSpec((chunk, ML_QK_WIDTH), row),
                  pl.BlockSpec((chunk, ML_WIDTH), row),
                  pl.BlockSpec((3 * SUBLANES, chunk), lambda b, c: (0, b * nc + c))],
        out_specs=pl.BlockSpec((chunk, ML_WIDTH), row),
        scratch_shapes=[pltpu.VMEM((ML_HEADS // 2, LANES, 2 * LANES), F32),
                        pltpu.VMEM((SUBLANES, LANES), F32)],
        compiler_params=_params("parallel", "arbitrary"), name="mlstm",
    )(q, k, v, gs)


def _attn_kernel(q_ref, k_ref, vt_ref, out_ref, s_sc, smax_sc, m_sc, acc_sc, *, tb):
    last = pl.program_id(2)
    heads = range(2)
    ones_rows = jnp.where(lax.broadcasted_iota(jnp.int32, (16, tb), 0) == 0, 1.0, 0.0).astype(BF16)

    def produce(j, slot, h, diagonal):
        kb = k_ref[pl.ds(pl.multiple_of(j * tb, tb), tb), h * HEAD_BLOCK:(h + 1) * HEAD_BLOCK]
        s = _dot_nt(kb, q_ref[:, h * HEAD_BLOCK:(h + 1) * HEAD_BLOCK])
        if diagonal:
            key = lax.broadcasted_iota(jnp.int32, s.shape, 0)
            qry = lax.broadcasted_iota(jnp.int32, s.shape, 1)
            s = jnp.where(key <= qry, s, -jnp.inf)
        s_sc[slot, h] = s
        smax_sc[slot, h] = jnp.max(s, axis=0, keepdims=True)

    def consume(j, slot, h):
        m = m_sc[h]
        m_new = jnp.maximum(m, smax_sc[slot, h])
        p = jnp.exp2(s_sc[slot, h] - m_new).astype(BF16)
        v_ext = jnp.concatenate([vt_ref[j, h * MLA_V:(h + 1) * MLA_V, :], ones_rows], axis=0)
        acc_sc[h] = jnp.exp2(m - m_new) * acc_sc[h] + _dot(v_ext, p)
        m_sc[h] = m_new

    def stage(j, slot, next_is_diagonal):
        for h in heads:
            produce(j + 1, 1 - slot, h, next_is_diagonal)
            consume(j, slot, h)

    def finish(slot):
        for h in heads:
            consume(last, slot, h)
        o_t = jnp.concatenate([acc_sc[h, :MLA_V, :] / acc_sc[h, MLA_V:MLA_V + 1, :] for h in heads], axis=0)
        out_ref[...] = o_t.T.astype(out_ref.dtype)

    m_sc[...] = jnp.full(m_sc.shape, -jnp.inf, F32)
    acc_sc[...] = jnp.zeros(acc_sc.shape, F32)

    @pl.when(last == 0)
    def _():
        for h in heads:
            produce(0, 0, h, True)
        finish(0)

    @pl.when(last > 0)
    def _():
        for h in heads:
            produce(0, 0, h, False)

        def two_stages(t, carry):
            stage(2 * t, 0, False)
            stage(2 * t + 1, 1, False)
            return carry

        lax.fori_loop(0, (last - 1) // 2, two_stages, 0)

        @pl.when(last % 2 == 1)
        def _():
            stage(last - 1, 0, True)
            finish(1)

        @pl.when(last % 2 == 0)
        def _():
            stage(last - 2, 0, False)
            stage(last - 1, 1, True)
            finish(0)


def _attention(q, k, vt, *, batch, seq, tb):
    t = q.shape[0]
    nb = seq // tb
    pairs = MLA_HEADS // 2
    return pl.pallas_call(
        functools.partial(_attn_kernel, tb=tb),
        out_shape=jax.ShapeDtypeStruct((t, MLA_WIDTH), BF16),
        grid=(batch, pairs, nb),
        in_specs=[pl.BlockSpec((tb, 2 * HEAD_BLOCK), lambda b, p, i: (b * nb + i, p)),
                  pl.BlockSpec((seq, 2 * HEAD_BLOCK), lambda b, p, i: (b, p)),
                  pl.BlockSpec((None, nb, 2 * MLA_V, tb), lambda b, p, i: (b, 0, p, 0))],
        out_specs=pl.BlockSpec((tb, 2 * MLA_V), lambda b, p, i: (b * nb + i, p)),
        scratch_shapes=[pltpu.VMEM((2, 2, tb, tb), F32), pltpu.VMEM((2, 2, 1, tb), F32),
                        pltpu.VMEM((2, 1, tb), F32), pltpu.VMEM((2, MLA_V + 16, tb), F32)],
        compiler_params=_params("parallel", "parallel", "arbitrary"), name="attn",
    )(q, k, vt)


def _out_proj_kernel(hml_ref, o_ref, hmla_ref, x_ref, hn_ref, wa_ref, wb_ref, gpost_ref, gpre_ref, x1_ref, m_ref):
    heads = []
    for h in range(ML_HEADS):
        hs = slice(h * ML_V, (h + 1) * ML_V)
        hh = _rms(hml_ref[:, hs].astype(F32), hn_ref[:, hs]) * jax.nn.sigmoid(o_ref[:, hs].astype(F32))
        heads.append(hh.astype(BF16))
    mix = _dot(jnp.concatenate(heads, axis=1), wa_ref[...]) + _dot(hmla_ref[...], wb_ref[...])
    x1 = x_ref[...] + _rms(mix, gpost_ref[...])
    x1_ref[...] = x1
    m_ref[...] = _rms(x1, gpre_ref[...]).astype(m_ref.dtype)


def _out_proj(hml, o, hmla, x2, hn, wa, wb, gpost, gpre, *, tm):
    t, d = x2.shape
    row = lambda i: (i, 0)
    const = lambda i: (0, 0)
    full = lambda arr: pl.BlockSpec(arr.shape, const)
    return pl.pallas_call(
        _out_proj_kernel,
        out_shape=(jax.ShapeDtypeStruct((t, d), F32), jax.ShapeDtypeStruct((t, d), BF16)),
        grid=(t // tm,),
        in_specs=[pl.BlockSpec((tm, ML_WIDTH), row), pl.BlockSpec((tm, ML_WIDTH), row),
                  pl.BlockSpec((tm, MLA_WIDTH), row), pl.BlockSpec((tm, d), row),
                  full(hn), full(wa), full(wb), full(gpost), full(gpre)],
        out_specs=(pl.BlockSpec((tm, d), row), pl.BlockSpec((tm, d), row)),
        compiler_params=_params("parallel"), name="out_proj",
    )(hml, o, hmla, x2, hn, wa, wb, gpost, gpre)


def _ffn_kernel(m_ref, x_ref, wu_ref, wd_ref, g_ref, out_ref, acc_ref):
    j = pl.program_id(1)
    h = jnp.square(jnp.maximum(_dot(m_ref[...], wu_ref[...]), 0.0)).astype(BF16)
    part = _dot(h, wd_ref[...])

    @pl.when(j == 0)
    def _():
        acc_ref[...] = part

    @pl.when(j > 0)
    def _():
        acc_ref[...] += part

    @pl.when(j == pl.num_programs(1) - 1)
    def _():
        out_ref[...] = x_ref[...] + _rms(acc_ref[...], g_ref[...])


def _ffn(m, x1, wu, wd, g, *, tm, tf):
    t, d = x1.shape
    dff = wu.shape[1]
    return pl.pallas_call(
        _ffn_kernel,
        out_shape=jax.ShapeDtypeStruct((t, d), F32),
        grid=(t // tm, dff // tf),
        in_specs=[pl.BlockSpec((tm, d), lambda i, j: (i, 0)), pl.BlockSpec((tm, d), lambda i, j: (i, 0)),
                  pl.BlockSpec((d, tf), lambda i, j: (0, j)), pl.BlockSpec((tf, d), lambda i, j: (j, 0)),
                  pl.BlockSpec(g.shape, lambda i, j: (0, 0))],
        out_specs=pl.BlockSpec((tm, d), lambda i, j: (i, 0)),
        scratch_shapes=[pltpu.VMEM((tm, d), F32)],
        compiler_params=_params("parallel", "arbitrary"), name="ffn",
    )(m, x1, wu, wd, g)


def _layout_in_proj(w_in, b_gates):
    d = w_in.shape[0]
    o0 = QK_COLS
    o1 = o0 + ML_WIDTH
    o2 = o1 + ML_WIDTH
    o3 = o2 + 2 * ML_HEADS
    o4 = o3 + MLA_Q_RANK
    o5 = o4 + MLA_KV_RANK
    gates, k_r = w_in[:, o2:o3], w_in[:, o5:]
    lead = jnp.zeros((d, ROPE_LANE0), w_in.dtype)
    pad = jnp.zeros((d, LANES - ROPE_LANE1), w_in.dtype)
    blka = jnp.concatenate([lead, k_r, pad], axis=1)
    blkb = jnp.concatenate([lead, -k_r[:, HALF_ROPE:], k_r[:, :HALF_ROPE], pad], axis=1)
    win = jnp.concatenate([w_in[:, :o2], w_in[:, o3:o5], blka, blkb], axis=1).astype(BF16)
    zrows = jnp.zeros((SUBLANES - ML_HEADS, d), w_in.dtype)
    wg = jnp.concatenate([gates[:, :ML_HEADS].T, zrows, gates[:, ML_HEADS:].T, zrows], axis=0).astype(BF16)
    zb = jnp.zeros((SUBLANES - ML_HEADS,), b_gates.dtype)
    gbias = jnp.concatenate([b_gates[:ML_HEADS], zb, b_gates[ML_HEADS:], zb]).reshape(2 * SUBLANES, 1)
    return win, wg, gbias


def _layout_mla(w_uq, w_ukv):
    rq, rkv = w_uq.shape[0], w_ukv.shape[0]
    wq3 = w_uq.reshape(rq, MLA_HEADS, MLA_NOPE + MLA_ROPE)
    nope, rope = wq3[:, :, :MLA_NOPE], wq3[:, :, MLA_NOPE:]
    pad = jnp.zeros((rq, MLA_HEADS, HEAD_BLOCK - ROPE_LANE1), w_uq.dtype)
    lead = jnp.zeros((rq, MLA_HEADS, ROPE_LANE0), w_uq.dtype)
    wq = jnp.concatenate([nope, rope, pad], axis=2).reshape(rq, MLA_HEADS * HEAD_BLOCK).astype(BF16)
    wqs = jnp.concatenate([lead, -rope[:, :, HALF_ROPE:], rope[:, :, :HALF_ROPE], pad], axis=2)
    wqs = wqs.reshape(rq, MLA_HEADS * HEAD_BLOCK).astype(BF16)
    wkv3 = w_ukv.reshape(rkv, MLA_HEADS, MLA_NOPE + MLA_V)
    kpad = jnp.zeros((rkv, MLA_HEADS, HEAD_BLOCK - MLA_NOPE), w_ukv.dtype)
    wk = jnp.concatenate([wkv3[:, :, :MLA_NOPE], kpad], axis=2).reshape(rkv, MLA_HEADS * HEAD_BLOCK).astype(BF16)
    wvt = wkv3[:, :, MLA_NOPE:].reshape(rkv, MLA_WIDTH).T.astype(BF16)
    return wq, wqs, wk, wvt


def kernel(x, positions, norm_pre_mix, w_in, b_gates, conv_w, conv_b, ml_head_norm, q_norm, w_uq, kv_norm, w_ukv,
           w_out, norm_post_mix, norm_pre_mlp, w_up, w_down, norm_post_mlp):
    batch, seq, d = x.shape
    depth = w_in.shape[0]
    t = batch * seq
    tm_in = min(512, seq)
    tb = min(512, seq)
    chunk = 256
    tm_out = min(512, t)
    tm_ffn = min(1024, t)
    tf = 1024

    cos_t, sin_t = _rope_tables(positions, tm_in)
    x2 = x.reshape(t, d)
    row = lambda v: v.reshape(1, -1)
    for l in range(depth):
        win, wg, gbias = _layout_in_proj(w_in[l], b_gates[l])
        wq, wqs, wk, wvt = _layout_mla(w_uq[l], w_ukv[l])
        qml, kml, v, o, gs, q, k, vt = _in_proj(
            x2, cos_t, sin_t, row(norm_pre_mix[l]), win, wg, gbias, conv_w[l], row(conv_b[l]),
            row(q_norm[l]), wq, wqs, row(kv_norm[l]), wk, wvt,
            batch=batch, seq=seq, tm=tm_in, tk=tb, chunk=chunk)
        hml = _mlstm(qml, kml, v, gs, batch=batch, seq=seq, chunk=chunk)
        hmla = _attention(q, k, vt, batch=batch, seq=seq, tb=tb)
        wo = w_out[l].astype(BF16)
        x1, m = _out_proj(hml, o, hmla, x2, row(ml_head_norm[l]), wo[:ML_WIDTH], wo[ML_WIDTH:],
                          row(norm_post_mix[l]), row(norm_pre_mlp[l]), tm=tm_out)
        x2 = _ffn(m, x1, w_up[l].astype(BF16), w_down[l].astype(BF16), row(norm_post_mlp[l]), tm=tm_ffn, tf=tf)
    return x2.reshape(batch, seq, d)
```

```python
import functools

import jax
import jax.numpy as jnp
from jax import lax
from jax.experimental import pallas as pl
from jax.experimental.pallas import tpu as pltpu

F32 = jnp.float32
BF16 = jnp.bfloat16

EPS = 1e-6
ML_HEADS, ML_QK, ML_V, ML_CONV = 4, 64, 128, 4
MLA_HEADS, MLA_NOPE, MLA_ROPE, MLA_V = 8, 64, 32, 64
MLA_Q_RANK, MLA_KV_RANK = 256, 128
ROPE_THETA = 10000.0
LOG2_E = 1.4426950408889634

ML_WIDTH = ML_HEADS * ML_V
ML_QK_WIDTH = ML_HEADS * ML_QK
QK_COLS = 2 * ML_QK_WIDTH
MLA_WIDTH = MLA_HEADS * MLA_V

LANES = 128
SUBLANES = 8
HEAD_BLOCK = LANES
ROPE_LANE0 = MLA_NOPE
ROPE_LANE1 = MLA_NOPE + MLA_ROPE
HALF_ROPE = MLA_ROPE // 2
C_QK, C_V, C_O = 0, QK_COLS, QK_COLS + ML_WIDTH
C_CQ = C_O + ML_WIDTH
C_CKV = C_CQ + MLA_Q_RANK
C_BLKA = C_CKV + MLA_KV_RANK
C_BLKB = C_BLKA + LANES
IN_COLS = C_BLKB + LANES
CONV_HALO = SUBLANES

VMEM_LIMIT = 56 * 1024 * 1024


def _params(*sem):
    return pltpu.CompilerParams(dimension_semantics=sem, vmem_limit_bytes=VMEM_LIMIT)


def _rms(x, g):
    return x * lax.rsqrt(jnp.mean(x * x, axis=-1, keepdims=True) + EPS) * g


def _log_sigmoid(x):
    return -(jnp.maximum(-x, 0.0) + jnp.log1p(jnp.exp(-jnp.abs(x))))


def _dot(a, b):
    return jnp.dot(a, b, preferred_element_type=F32)


def _dot_nt(a, b):
    return lax.dot_general(a, b, (((1,), (1,)), ((), ())), preferred_element_type=F32)


def _dot_tn(a, b):
    return lax.dot_general(a, b, (((0,), (0,)), ((), ())), preferred_element_type=F32)


def _rope_table_kernel(pos_ref, inv_ref, cos_ref, sin_ref):
    ang = pos_ref[...].astype(F32) * inv_ref[...]
    lane = lax.broadcasted_iota(jnp.int32, ang.shape, 1)
    rope = (lane >= ROPE_LANE0) & (lane < ROPE_LANE1)
    cos_ref[...] = jnp.where(lane < ROPE_LANE0, 1.0, jnp.where(rope, jnp.cos(ang), 0.0))
    sin_ref[...] = jnp.where(rope, jnp.sin(ang), 0.0)


def _rope_tables(positions, tm):
    t = positions.size
    inv = 1.0 / (ROPE_THETA ** (jnp.arange(0, MLA_ROPE, 2, dtype=F32) / MLA_ROPE))
    inv_row = jnp.zeros((1, LANES), F32)
    inv_row = inv_row.at[0, ROPE_LANE0:ROPE_LANE0 + HALF_ROPE].set(inv)
    inv_row = inv_row.at[0, ROPE_LANE0 + HALF_ROPE:ROPE_LANE1].set(inv)
    return pl.pallas_call(
        _rope_table_kernel,
        out_shape=(jax.ShapeDtypeStruct((t, LANES), F32),) * 2,
        grid=(t // tm,),
        in_specs=[pl.BlockSpec((tm, 1), lambda i: (i, 0)),
                  pl.BlockSpec((1, LANES), lambda i: (0, 0))],
        out_specs=(pl.BlockSpec((tm, LANES), lambda i: (i, 0)),) * 2,
        compiler_params=_params("parallel"),
        name="rope_tables",
    )(positions.reshape(t, 1), inv_row)


def _scan_lanes(x, op, identity, segment):
    pos = lax.broadcasted_iota(jnp.int32, x.shape, 1) % segment
    step = 1
    while step < segment:
        x = op(x, jnp.where(pos >= step, pltpu.roll(x, step, axis=1), identity))
        step *= 2
    return x


def _in_proj_kernel(x_ref, cos_ref, sin_ref, g_ref, win_ref, wg_ref, gbias_ref, cw_ref, cb_ref, qn_ref, wq_ref,
                    wqs_ref, kvn_ref, wk_ref, wvt_ref,
                    qml_ref, kml_ref, v_ref, o_ref, gs_ref, q_ref, k_ref, vt_ref, xbuf,
                    *, tiles_per_batch, chunk, sub):
    tm = x_ref.shape[0]
    scale = (MLA_NOPE + MLA_ROPE) ** -0.5 * LOG2_E

    @pl.when(pl.program_id(0) % tiles_per_batch == 0)
    def _():
        xbuf[0:CONV_HALO, :] = jnp.zeros((CONV_HALO, QK_COLS), F32)

    blocks = [slice(r, r + sub) for r in range(0, tm, sub)]
    acts = [_rms(x_ref[rows, :], g_ref[...]).astype(BF16) for rows in blocks]
    projs = [_dot(a, win_ref[...]) for a in acts]

    for rows, a, proj in zip(blocks, acts, projs):
        v_ref[rows, :] = proj[:, C_V:C_O].astype(BF16)
        o_ref[rows, :] = proj[:, C_O:C_CQ].astype(BF16)

        top = CONV_HALO + rows.start
        xbuf[top:top + sub, :] = proj[:, C_QK:C_V]
        cw = cw_ref[...]
        y = cb_ref[...]
        for tap in range(ML_CONV):
            start = top - (ML_CONV - 1 - tap)
            y = y + xbuf[start:start + sub, :] * cw[tap:tap + 1, :]
        qk = y * jax.nn.sigmoid(y)
        qml_ref[rows, :] = qk[:, :ML_QK_WIDTH].astype(BF16)
        kml_ref[rows, :] = (qk[:, ML_QK_WIDTH:] * (ML_QK ** -0.5)).astype(BF16)

        gates = _dot_nt(wg_ref[...], a) + gbias_ref[...]
        b = _scan_lanes(_log_sigmoid(gates[SUBLANES:]), jnp.add, 0.0, chunk)
        g = gates[:SUBLANES] - b
        gs_ref[:, rows] = jnp.concatenate([b, g, _scan_lanes(g, jnp.maximum, -jnp.inf, chunk)], axis=0)

        cos_t = cos_ref[rows, :]
        sin_t = sin_ref[rows, :]
        qn = _rms(proj[:, C_CQ:C_CKV], qn_ref[...]).astype(BF16)
        qf = _dot(qn, wq_ref[...])
        qs = _dot(qn, wqs_ref[...])
        ckvn = _rms(proj[:, C_CKV:C_BLKA], kvn_ref[...]).astype(BF16)
        kf = _dot(ckvn, wk_ref[...])
        k_rope = proj[:, C_BLKA:C_BLKB] * cos_t + proj[:, C_BLKB:IN_COLS] * sin_t
        for h in range(MLA_HEADS):
            blk = slice(h * HEAD_BLOCK, (h + 1) * HEAD_BLOCK)
            q_ref[rows, blk] = ((qf[:, blk] * cos_t + qs[:, blk] * sin_t) * scale).astype(BF16)
            k_ref[rows, blk] = (kf[:, blk] + k_rope).astype(BF16)

        vt = _dot_nt(wvt_ref[...], ckvn).astype(BF16)
        if len(vt_ref.shape) == 2:
            vt_ref[:, rows] = vt
        else:
            tk = vt_ref.shape[-1]
            for j in range(sub // tk):
                vt_ref[rows.start // tk + j] = vt[:, j * tk:(j + 1) * tk]

    xbuf[0:CONV_HALO, :] = xbuf[tm:tm + CONV_HALO, :]


def _in_proj(x2, cos_t, sin_t, g, win, wg, gbias, conv_w, conv_b, qn, wq, wqs, kvn, wk, wvt,
             *, batch, seq, tm, tk, chunk):
    t, d = x2.shape
    nt = t // tm
    tiles_per_batch = seq // tm
    if tk >= tm:
        per_kv = tk // tm
        vt_spec = pl.BlockSpec((None, None, MLA_WIDTH, tm),
                               lambda i: (i // tiles_per_batch, (i % tiles_per_batch) // per_kv, 0, i % per_kv))
    else:
        vt_spec = pl.BlockSpec((None, tm // tk, MLA_WIDTH, tk),
                               lambda i: (i // tiles_per_batch, i % tiles_per_batch, 0, 0))
    row = lambda i: (i, 0)
    col = lambda i: (0, i)
    const = lambda i: (0, 0)
    full = lambda arr: pl.BlockSpec(arr.shape, const)
    out_shape = (
        jax.ShapeDtypeStruct((t, ML_QK_WIDTH), BF16),
        jax.ShapeDtypeStruct((t, ML_QK_WIDTH), BF16),
        jax.ShapeDtypeStruct((t, ML_WIDTH), BF16),
        jax.ShapeDtypeStruct((t, ML_WIDTH), BF16),
        jax.ShapeDtypeStruct((3 * SUBLANES, t), F32),
        jax.ShapeDtypeStruct((t, MLA_HEADS * HEAD_BLOCK), BF16),
        jax.ShapeDtypeStruct((t, MLA_HEADS * HEAD_BLOCK), BF16),
        jax.ShapeDtypeStruct((batch, seq // tk, MLA_WIDTH, tk), BF16),
    )
    out_specs = (
        pl.BlockSpec((tm, ML_QK_WIDTH), row),
        pl.BlockSpec((tm, ML_QK_WIDTH), row),
        pl.BlockSpec((tm, ML_WIDTH), row),
        pl.BlockSpec((tm, ML_WIDTH), row),
        pl.BlockSpec((3 * SUBLANES, tm), col),
        pl.BlockSpec((tm, MLA_HEADS * HEAD_BLOCK), row),
        pl.BlockSpec((tm, MLA_HEADS * HEAD_BLOCK), row),
        vt_spec,
    )
    in_specs = [pl.BlockSpec((tm, d), row), pl.BlockSpec((tm, LANES), row), pl.BlockSpec((tm, LANES), row),
                full(g), full(win), full(wg), full(gbias), full(conv_w), full(conv_b),
                full(qn), full(wq), full(wqs), full(kvn), full(wk), full(wvt)]
    return pl.pallas_call(
        functools.partial(_in_proj_kernel, tiles_per_batch=tiles_per_batch, chunk=chunk, sub=min(tm, 256)),
        out_shape=out_shape, grid=(nt,), in_specs=in_specs, out_specs=out_specs,
        scratch_shapes=[pltpu.VMEM((tm + CONV_HALO, QK_COLS), F32)],
        compiler_params=_params("arbitrary"), name="in_proj",
    )(x2, cos_t, sin_t, g, win, wg, gbias, conv_w, conv_b, qn, wq, wqs, kvn, wk, wvt)


def _mlstm_kernel(q_ref, k_ref, v_ref, gs_ref, out_ref, c_ref, m_ref, *, chunk):
    L = chunk

    @pl.when(pl.program_id(1) == 0)
    def _():
        c_ref[...] = jnp.zeros_like(c_ref)
        m_ref[...] = jnp.zeros_like(m_ref)

    b, g, big_g = gs_ref[0:SUBLANES, :], gs_ref[SUBLANES:2 * SUBLANES, :], gs_ref[2 * SUBLANES:, :]
    m_prev = jnp.concatenate([m_ref[...]] * (L // LANES), axis=1)
    big_m = jnp.maximum(big_g, m_prev)
    inter_w = jnp.exp(m_prev - big_m)
    floor = jnp.exp(-(b + big_m))
    m_end = big_m[:, L - 1:L]
    w_state = jnp.exp(g - m_end)
    m_ref[...] = jnp.broadcast_to(b[:, L - 1:L] + m_end, m_ref.shape)
    stats = [big_m, inter_w, floor, w_state]
    pad = jnp.zeros((LANES - len(stats) * SUBLANES, L), F32)
    cols = jnp.concatenate(stats + [pad], axis=0).T

    row = lax.broadcasted_iota(jnp.int32, (L, L), 0)
    col = lax.broadcasted_iota(jnp.int32, (L, L), 1)
    causal = row >= col
    lane = lax.broadcasted_iota(jnp.int32, (L, LANES), 1)
    ones_blk = jnp.where(lane == 0, 1.0, 0.0).astype(BF16)

    for h in range(ML_HEADS):
        pair, lo = h // 2, (h % 2) * ML_QK
        blk = slice(pair * LANES, (pair + 1) * LANES)
        mine = (lane >= lo) & (lane < lo + ML_QK)
        qm = jnp.where(mine, q_ref[:, blk], jnp.zeros((), BF16))
        k_blk = k_ref[:, blk]
        s = _dot_nt(qm, k_blk)
        dw = jnp.where(causal, jnp.exp(g[h:h + 1, :] - cols[:, h:h + 1]), 0.0)
        v_ext = jnp.concatenate([v_ref[:, h * ML_V:(h + 1) * ML_V], ones_blk], axis=1)
        c_pair = c_ref[pair]
        iw = cols[:, SUBLANES + h:SUBLANES + h + 1]
        num = _dot((s * dw).astype(BF16), v_ext) + iw * _dot(qm, c_pair.astype(BF16))
        den = num[:, ML_V:ML_V + 1]
        hh = num[:, :ML_V] / jnp.maximum(jnp.abs(den), cols[:, 2 * SUBLANES + h:2 * SUBLANES + h + 1])
        out_ref[:, h * ML_V:(h + 1) * ML_V] = hh.astype(out_ref.dtype)

        kw = (k_blk.astype(F32) * cols[:, 3 * SUBLANES + h:3 * SUBLANES + h + 1]).astype(BF16)
        upd = _dot_tn(kw, v_ext)
        rows = slice(lo, lo + ML_QK)
        c_ref[pair, rows, :] = inter_w[h:h + 1, L - 1:L] * c_pair[rows, :] + upd[rows, :]


def _mlstm(q, k, v, gs, *, batch, seq, chunk):
    t = q.shape[0]
    nc = seq // chunk
    row = lambda b, c: (b * nc + c, 0)
    return pl.pallas_call(
        functools.partial(_mlstm_kernel, chunk=chunk),
        out_shape=jax.ShapeDtypeStruct((t, ML_WIDTH), BF16),
        grid=(batch, nc),
        in_specs=[pl.BlockSpec((chunk, ML_QK_WIDTH), row), pl.BlockSpec((chunk, ML_QK_WIDTH), row),
                  pl.BlockSpec((chunk, ML_WIDTH), row),
                  pl.BlockSpec((3 * SUBLANES, chunk), lambda b, c: (0, b * nc + c))],
        out_specs=pl.BlockSpec((chunk, ML_WIDTH), row),
        scratch_shapes=[pltpu.VMEM((ML_HEADS // 2, LANES, 2 * LANES), F32),
                        pltpu.VMEM((SUBLANES, LANES), F32)],
        compiler_params=_params("parallel", "arbitrary"), name="mlstm",
    )(q, k, v, gs)


def _attn_kernel(q_ref, k_ref, vt_ref, out_ref, s_sc, smax_sc, m_sc, acc_sc, *, tb, hp):
    last = pl.program_id(2)
    heads = range(hp)
    ones_rows = jnp.where(lax.broadcasted_iota(jnp.int32, (16, tb), 0) == 0, 1.0, 0.0).astype(BF16)

    def produce(j, slot, h, diagonal):
        kb = k_ref[pl.ds(pl.multiple_of(j * tb, tb), tb), h * HEAD_BLOCK:(h + 1) * HEAD_BLOCK]
        s = _dot_nt(kb, q_ref[:, h * HEAD_BLOCK:(h + 1) * HEAD_BLOCK])
        if diagonal:
            key = lax.broadcasted_iota(jnp.int32, s.shape, 0)
            qry = lax.broadcasted_iota(jnp.int32, s.shape, 1)
            s = jnp.where(key <= qry, s, -jnp.inf)
        s_sc[slot, h] = s
        smax_sc[slot, h] = jnp.max(s, axis=0, keepdims=True)

    def consume(j, slot, h):
        m = m_sc[h]
        m_new = jnp.maximum(m, smax_sc[slot, h])
        p = jnp.exp2(s_sc[slot, h] - m_new).astype(BF16)
        v_ext = jnp.concatenate([vt_ref[j, h * MLA_V:(h + 1) * MLA_V, :], ones_rows], axis=0)
        acc_sc[h] = jnp.exp2(m - m_new) * acc_sc[h] + _dot(v_ext, p)
        m_sc[h] = m_new

    def stage(j, slot, next_is_diagonal):
        for h in heads:
            produce(j + 1, 1 - slot, h, next_is_diagonal)
            consume(j, slot, h)

    def finish(slot):
        for h in heads:
            consume(last, slot, h)
        o_t = jnp.concatenate([acc_sc[h, :MLA_V, :] / acc_sc[h, MLA_V:MLA_V + 1, :] for h in heads], axis=0)
        out_ref[...] = o_t.T.astype(out_ref.dtype)

    m_sc[...] = jnp.full(m_sc.shape, -jnp.inf, F32)
    acc_sc[...] = jnp.zeros(acc_sc.shape, F32)

    @pl.when(last == 0)
    def _():
        for h in heads:
            produce(0, 0, h, True)
        finish(0)

    @pl.when(last > 0)
    def _():
        for h in heads:
            produce(0, 0, h, False)

        def two_stages(t, carry):
            stage(2 * t, 0, False)
            stage(2 * t + 1, 1, False)
            return carry

        lax.fori_loop(0, (last - 1) // 2, two_stages, 0)

        @pl.when(last % 2 == 1)
        def _():
            stage(last - 1, 0, True)
            finish(1)

        @pl.when(last % 2 == 0)
        def _():
            stage(last - 2, 0, False)
            stage(last - 1, 1, True)
            finish(0)


def _attention(q, k, vt, *, batch, seq, tb, hp):
    t = q.shape[0]
    nb = seq // tb
    return pl.pallas_call(
        functools.partial(_attn_kernel, tb=tb, hp=hp),
        out_shape=jax.ShapeDtypeStruct((t, MLA_WIDTH), BF16),
        grid=(batch, MLA_HEADS // hp, nb),
        in_specs=[pl.BlockSpec((tb, hp * HEAD_BLOCK), lambda b, p, i: (b * nb + i, p)),
                  pl.BlockSpec((seq, hp * HEAD_BLOCK), lambda b, p, i: (b, p)),
                  pl.BlockSpec((None, nb, hp * MLA_V, tb), lambda b, p, i: (b, 0, p, 0))],
        out_specs=pl.BlockSpec((tb, hp * MLA_V), lambda b, p, i: (b * nb + i, p)),
        scratch_shapes=[pltpu.VMEM((2, hp, tb, tb), F32), pltpu.VMEM((2, hp, 1, tb), F32),
                        pltpu.VMEM((hp, 1, tb), F32), pltpu.VMEM((hp, MLA_V + 16, tb), F32)],
        compiler_params=_params("parallel", "parallel", "arbitrary"), name="attn",
    )(q, k, vt)


def _mix_ffn_kernel(hml_ref, o_ref, hmla_ref, x_ref, hn_ref, wa_ref, wb_ref, gpost_ref, gpre_ref,
                    wu_ref, wd_ref, gmlp_ref, out_ref, m_sc, acc_sc):
    j = pl.program_id(1)

    @pl.when(j == 0)
    def _():
        heads = []
        for h in range(ML_HEADS):
            hs = slice(h * ML_V, (h + 1) * ML_V)
            hh = _rms(hml_ref[:, hs].astype(F32), hn_ref[:, hs]) * jax.nn.sigmoid(o_ref[:, hs].astype(F32))
            heads.append(hh.astype(BF16))
        mix = _dot(jnp.concatenate(heads, axis=1), wa_ref[...]) + _dot(hmla_ref[...], wb_ref[...])
        x1 = x_ref[...] + _rms(mix, gpost_ref[...])
        out_ref[...] = x1
        m_sc[...] = _rms(x1, gpre_ref[...]).astype(m_sc.dtype)

    h = jnp.square(jnp.maximum(_dot(m_sc[...], wu_ref[...]), 0.0)).astype(BF16)
    part = _dot(h, wd_ref[...])

    @pl.when(j == 0)
    def _():
        acc_sc[...] = part

    @pl.when(j > 0)
    def _():
        acc_sc[...] += part

    @pl.when(j == pl.num_programs(1) - 1)
    def _():
        out_ref[...] += _rms(acc_sc[...], gmlp_ref[...])


def _mix_ffn(hml, o, hmla, x2, hn, wa, wb, gpost, gpre, wu, wd, gmlp, *, tm, tf):
    t, d = x2.shape
    dff = wu.shape[1]
    row = lambda i, j: (i, 0)
    const = lambda i, j: (0, 0)
    full = lambda arr: pl.BlockSpec(arr.shape, const)
    return pl.pallas_call(
        _mix_ffn_kernel,
        out_shape=jax.ShapeDtypeStruct((t, d), F32),
        grid=(t // tm, dff // tf),
        in_specs=[pl.BlockSpec((tm, ML_WIDTH), row), pl.BlockSpec((tm, ML_WIDTH), row),
                  pl.BlockSpec((tm, MLA_WIDTH), row), pl.BlockSpec((tm, d), row),
                  full(hn), full(wa), full(wb), full(gpost), full(gpre),
                  pl.BlockSpec((d, tf), lambda i, j: (0, j)), pl.BlockSpec((tf, d), lambda i, j: (j, 0)), full(gmlp)],
        out_specs=pl.BlockSpec((tm, d), row),
        scratch_shapes=[pltpu.VMEM((tm, d), BF16), pltpu.VMEM((tm, d), F32)],
        compiler_params=_params("parallel", "arbitrary"), name="mix_ffn",
    )(hml, o, hmla, x2, hn, wa, wb, gpost, gpre, wu, wd, gmlp)


def _layout_in_proj(w_in, b_gates):
    d = w_in.shape[0]
    o0 = QK_COLS
    o1 = o0 + ML_WIDTH
    o2 = o1 + ML_WIDTH
    o3 = o2 + 2 * ML_HEADS
    o4 = o3 + MLA_Q_RANK
    o5 = o4 + MLA_KV_RANK
    gates, k_r = w_in[:, o2:o3], w_in[:, o5:]
    lead = jnp.zeros((d, ROPE_LANE0), w_in.dtype)
    pad = jnp.zeros((d, LANES - ROPE_LANE1), w_in.dtype)
    blka = jnp.concatenate([lead, k_r, pad], axis=1)
    blkb = jnp.concatenate([lead, -k_r[:, HALF_ROPE:], k_r[:, :HALF_ROPE], pad], axis=1)
    win = jnp.concatenate([w_in[:, :o2], w_in[:, o3:o5], blka, blkb], axis=1).astype(BF16)
    zrows = jnp.zeros((SUBLANES - ML_HEADS, d), w_in.dtype)
    wg = jnp.concatenate([gates[:, :ML_HEADS].T, zrows, gates[:, ML_HEADS:].T, zrows], axis=0).astype(BF16)
    zb = jnp.zeros((SUBLANES - ML_HEADS,), b_gates.dtype)
    gbias = jnp.concatenate([b_gates[:ML_HEADS], zb, b_gates[ML_HEADS:], zb]).reshape(2 * SUBLANES, 1)
    return win, wg, gbias


def _layout_mla(w_uq, w_ukv):
    rq, rkv = w_uq.shape[0], w_ukv.shape[0]
    wq3 = w_uq.reshape(rq, MLA_HEADS, MLA_NOPE + MLA_ROPE)
    nope, rope = wq3[:, :, :MLA_NOPE], wq3[:, :, MLA_NOPE:]
    pad = jnp.zeros((rq, MLA_HEADS, HEAD_BLOCK - ROPE_LANE1), w_uq.dtype)
    lead = jnp.zeros((rq, MLA_HEADS, ROPE_LANE0), w_uq.dtype)
    wq = jnp.concatenate([nope, rope, pad], axis=2).reshape(rq, MLA_HEADS * HEAD_BLOCK).astype(BF16)
    wqs = jnp.concatenate([lead, -rope[:, :, HALF_ROPE:], rope[:, :, :HALF_ROPE], pad], axis=2)
    wqs = wqs.reshape(rq, MLA_HEADS * HEAD_BLOCK).astype(BF16)
    wkv3 = w_ukv.reshape(rkv, MLA_HEADS, MLA_NOPE + MLA_V)
    kpad = jnp.zeros((rkv, MLA_HEADS, HEAD_BLOCK - MLA_NOPE), w_ukv.dtype)
    wk = jnp.concatenate([wkv3[:, :, :MLA_NOPE], kpad], axis=2).reshape(rkv, MLA_HEADS * HEAD_BLOCK).astype(BF16)
    wvt = wkv3[:, :, MLA_NOPE:].reshape(rkv, MLA_WIDTH).T.astype(BF16)
    return wq, wqs, wk, wvt


def kernel(x, positions, norm_pre_mix, w_in, b_gates, conv_w, conv_b, ml_head_norm, q_norm, w_uq, kv_norm, w_ukv,
           w_out, norm_post_mix, norm_pre_mlp, w_up, w_down, norm_post_mlp):
    batch, seq, d = x.shape
    depth = w_in.shape[0]
    t = batch * seq
    tm_in = min(512, seq)
    tb = min(256, seq)
    hp = 8
    chunk = 256
    tm_ffn = min(1024, t)
    tf = 1024

    cos_t, sin_t = _rope_tables(positions, tm_in)
    x2 = x.reshape(t, d)
    row = lambda v: v.reshape(1, -1)
    for l in range(depth):
        win, wg, gbias = _layout_in_proj(w_in[l], b_gates[l])
        wq, wqs, wk, wvt = _layout_mla(w_uq[l], w_ukv[l])
        qml, kml, v, o, gs, q, k, vt = _in_proj(
            x2, cos_t, sin_t, row(norm_pre_mix[l]), win, wg, gbias, conv_w[l], row(conv_b[l]),
            row(q_norm[l]), wq, wqs, row(kv_norm[l]), wk, wvt,
            batch=batch, seq=seq, tm=tm_in, tk=tb, chunk=chunk)
        hml = _mlstm(qml, kml, v, gs, batch=batch, seq=seq, chunk=chunk)
        hmla = _attention(q, k, vt, batch=batch, seq=seq, tb=tb, hp=hp)
        wo = w_out[l].astype(BF16)
        x2 = _mix_ffn(hml, o, hmla, x2, row(ml_head_norm[l]), wo[:ML_WIDTH], wo[ML_WIDTH:],
                      row(norm_post_mix[l]), row(norm_pre_mlp[l]), w_up[l].astype(BF16), w_down[l].astype(BF16),
                      row(norm_post_mlp[l]), tm=tm_ffn, tf=tf)
    return x2.reshape(batch, seq, d)
```

```python
import functools

import jax
import jax.numpy as jnp
from jax import lax
from jax.experimental import pallas as pl
from jax.experimental.pallas import tpu as pltpu

F32 = jnp.float32
BF16 = jnp.bfloat16

EPS = 1e-6
ML_HEADS, ML_QK, ML_V, ML_CONV = 4, 64, 128, 4
MLA_HEADS, MLA_NOPE, MLA_ROPE, MLA_V = 8, 64, 32, 64
MLA_Q_RANK, MLA_KV_RANK = 256, 128
ROPE_THETA = 10000.0
LOG2_E = 1.4426950408889634

ML_WIDTH = ML_HEADS * ML_V
ML_QK_WIDTH = ML_HEADS * ML_QK
QK_COLS = 2 * ML_QK_WIDTH
MLA_WIDTH = MLA_HEADS * MLA_V

LANES = 128
SUBLANES = 8
HEAD_BLOCK = LANES
ROPE_LANE0 = MLA_NOPE
ROPE_LANE1 = MLA_NOPE + MLA_ROPE
HALF_ROPE = MLA_ROPE // 2
C_QK, C_V, C_O = 0, QK_COLS, QK_COLS + ML_WIDTH
C_CQ = C_O + ML_WIDTH
C_CKV = C_CQ + MLA_Q_RANK
C_BLKA = C_CKV + MLA_KV_RANK
C_BLKB = C_BLKA + LANES
IN_COLS = C_BLKB + LANES
CONV_HALO = SUBLANES

VMEM_LIMIT = 56 * 1024 * 1024


def _params(*sem):
    return pltpu.CompilerParams(dimension_semantics=sem, vmem_limit_bytes=VMEM_LIMIT)


def _rms(x, g):
    return x * lax.rsqrt(jnp.mean(x * x, axis=-1, keepdims=True) + EPS) * g


def _log_sigmoid(x):
    return -(jnp.maximum(-x, 0.0) + jnp.log1p(jnp.exp(-jnp.abs(x))))


def _dot(a, b):
    return jnp.dot(a, b, preferred_element_type=F32)


def _dot_nt(a, b):
    return lax.dot_general(a, b, (((1,), (1,)), ((), ())), preferred_element_type=F32)


def _dot_tn(a, b):
    return lax.dot_general(a, b, (((0,), (0,)), ((), ())), preferred_element_type=F32)


def _rope_table_kernel(pos_ref, inv_ref, cos_ref, sin_ref):
    pos = pos_ref[...].astype(F32)
    slot = lax.broadcasted_iota(jnp.int32, cos_ref.shape, 1) // HALF_ROPE
    ang = jnp.zeros(cos_ref.shape, F32)
    for t in range(pos.shape[1]):
        ang = jnp.where(slot == t, pos[:, t:t + 1], ang)
    ang = ang * inv_ref[...]
    cos_ref[...] = jnp.cos(ang)
    sin_ref[...] = jnp.sin(ang)


def _rope_tables(positions):
    t = positions.size
    per_row = LANES // HALF_ROPE
    rows = t // per_row
    tr = min(rows, 512)
    inv = 1.0 / (ROPE_THETA ** (jnp.arange(0, MLA_ROPE, 2, dtype=F32) / MLA_ROPE))
    cos_c, sin_c = pl.pallas_call(
        _rope_table_kernel,
        out_shape=(jax.ShapeDtypeStruct((rows, LANES), F32),) * 2,
        grid=(rows // tr,),
        in_specs=[pl.BlockSpec((tr, per_row), lambda i: (i, 0)),
                  pl.BlockSpec((1, LANES), lambda i: (0, 0))],
        out_specs=(pl.BlockSpec((tr, LANES), lambda i: (i, 0)),) * 2,
        compiler_params=_params("parallel"),
        name="rope_tables",
    )(positions.reshape(rows, per_row), jnp.tile(inv, per_row).reshape(1, LANES))
    cos_h, sin_h = cos_c.reshape(t, HALF_ROPE), sin_c.reshape(t, HALF_ROPE)
    ones = jnp.ones((t, ROPE_LANE0), F32)
    zeros = jnp.zeros((t, ROPE_LANE0), F32)
    pad = jnp.zeros((t, LANES - ROPE_LANE1), F32)
    return (jnp.concatenate([ones, cos_h, cos_h, pad], axis=1), jnp.concatenate([zeros, sin_h, sin_h, pad], axis=1))


def _scan_lanes(x, op, identity, segment):
    pos = lax.broadcasted_iota(jnp.int32, x.shape, 1) % segment
    step = 1
    while step < segment:
        x = op(x, jnp.where(pos >= step, pltpu.roll(x, step, axis=1), identity))
        step *= 2
    return x


def _in_proj_kernel(x_ref, cos_ref, sin_ref, g_ref, win_ref, wg_ref, gbias_ref, cw_ref, cb_ref, qn_ref, wq_ref,
                    wqs_ref, kvn_ref, wk_ref, wvt_ref,
                    qml_ref, kml_ref, v_ref, o_ref, gs_ref, q_ref, k_ref, vt_ref, xbuf,
                    *, tiles_per_batch, chunk, sub):
    tm = x_ref.shape[0]
    scale = (MLA_NOPE + MLA_ROPE) ** -0.5 * LOG2_E

    @pl.when(pl.program_id(0) % tiles_per_batch == 0)
    def _():
        xbuf[0:CONV_HALO, :] = jnp.zeros((CONV_HALO, QK_COLS), F32)

    blocks = [slice(r, r + sub) for r in range(0, tm, sub)]
    acts = [_rms(x_ref[rows, :], g_ref[...]).astype(BF16) for rows in blocks]
    projs = [_dot(a, win_ref[...]) for a in acts]

    for rows, a, proj in zip(blocks, acts, projs):
        v_ref[rows, :] = proj[:, C_V:C_O].astype(BF16)
        o_ref[rows, :] = proj[:, C_O:C_CQ].astype(BF16)

        top = CONV_HALO + rows.start
        xbuf[top:top + sub, :] = proj[:, C_QK:C_V]
        cw = cw_ref[...]
        y = cb_ref[...]
        for tap in range(ML_CONV):
            start = top - (ML_CONV - 1 - tap)
            y = y + xbuf[start:start + sub, :] * cw[tap:tap + 1, :]
        qk = y * jax.nn.sigmoid(y)
        qml_ref[rows, :] = qk[:, :ML_QK_WIDTH].astype(BF16)
        kml_ref[rows, :] = (qk[:, ML_QK_WIDTH:] * (ML_QK ** -0.5)).astype(BF16)

        gates = _dot_nt(wg_ref[...], a) + gbias_ref[...]
        b = _scan_lanes(_log_sigmoid(gates[SUBLANES:]), jnp.add, 0.0, chunk)
        g = gates[:SUBLANES] - b
        gs_ref[:, rows] = jnp.concatenate([b, g, _scan_lanes(g, jnp.maximum, -jnp.inf, chunk)], axis=0)

        cos_t = cos_ref[rows, :]
        sin_t = sin_ref[rows, :]
        qn = _rms(proj[:, C_CQ:C_CKV], qn_ref[...]).astype(BF16)
        qf = _dot(qn, wq_ref[...])
        qs = _dot(qn, wqs_ref[...])
        ckvn = _rms(proj[:, C_CKV:C_BLKA], kvn_ref[...]).astype(BF16)
        kf = _dot(ckvn, wk_ref[...])
        k_rope = proj[:, C_BLKA:C_BLKB] * cos_t + proj[:, C_BLKB:IN_COLS] * sin_t
        for h in range(MLA_HEADS):
            blk = slice(h * HEAD_BLOCK, (h + 1) * HEAD_BLOCK)
            q_ref[rows, blk] = ((qf[:, blk] * cos_t + qs[:, blk] * sin_t) * scale).astype(BF16)
            k_ref[rows, blk] = (kf[:, blk] + k_rope).astype(BF16)

        vt = _dot_nt(wvt_ref[...], ckvn).astype(BF16)
        if len(vt_ref.shape) == 2:
            vt_ref[:, rows] = vt
        else:
            tk = vt_ref.shape[-1]
            for j in range(sub // tk):
                vt_ref[rows.start // tk + j] = vt[:, j * tk:(j + 1) * tk]

    xbuf[0:CONV_HALO, :] = xbuf[tm:tm + CONV_HALO, :]


def _in_proj(x2, cos_t, sin_t, params, layer, *, batch, seq, tm, tk, chunk):
    t, d = x2.shape
    nt = t // tm
    tiles_per_batch = seq // tm
    if tk >= tm:
        per_kv = tk // tm
        vt_spec = pl.BlockSpec((None, None, MLA_WIDTH, tm),
                               lambda i: (i // tiles_per_batch, (i % tiles_per_batch) // per_kv, 0, i % per_kv))
    else:
        vt_spec = pl.BlockSpec((None, tm // tk, MLA_WIDTH, tk),
                               lambda i: (i // tiles_per_batch, i % tiles_per_batch, 0, 0))
    row = lambda i: (i, 0)
    col = lambda i: (0, i)
    of_layer = lambda arr: pl.BlockSpec((None,) + arr.shape[1:], lambda i: (layer, 0, 0))
    out_shape = (
        jax.ShapeDtypeStruct((t, ML_QK_WIDTH), BF16),
        jax.ShapeDtypeStruct((t, ML_QK_WIDTH), BF16),
        jax.ShapeDtypeStruct((t, ML_WIDTH), BF16),
        jax.ShapeDtypeStruct((t, ML_WIDTH), BF16),
        jax.ShapeDtypeStruct((3 * SUBLANES, t), F32),
        jax.ShapeDtypeStruct((t, MLA_HEADS * HEAD_BLOCK), BF16),
        jax.ShapeDtypeStruct((t, MLA_HEADS * HEAD_BLOCK), BF16),
        jax.ShapeDtypeStruct((batch, seq // tk, MLA_WIDTH, tk), BF16),
    )
    out_specs = (
        pl.BlockSpec((tm, ML_QK_WIDTH), row),
        pl.BlockSpec((tm, ML_QK_WIDTH), row),
        pl.BlockSpec((tm, ML_WIDTH), row),
        pl.BlockSpec((tm, ML_WIDTH), row),
        pl.BlockSpec((3 * SUBLANES, tm), col),
        pl.BlockSpec((tm, MLA_HEADS * HEAD_BLOCK), row),
        pl.BlockSpec((tm, MLA_HEADS * HEAD_BLOCK), row),
        vt_spec,
    )
    in_specs = [pl.BlockSpec((tm, d), row), pl.BlockSpec((tm, LANES), row), pl.BlockSpec((tm, LANES), row)]
    in_specs += [of_layer(p) for p in params]
    return pl.pallas_call(
        functools.partial(_in_proj_kernel, tiles_per_batch=tiles_per_batch, chunk=chunk, sub=min(tm, 256)),
        out_shape=out_shape, grid=(nt,), in_specs=in_specs, out_specs=out_specs,
        scratch_shapes=[pltpu.VMEM((tm + CONV_HALO, QK_COLS), F32)],
        compiler_params=_params("arbitrary"), name="in_proj",
    )(x2, cos_t, sin_t, *params)


def _mlstm_kernel(q_ref, k_ref, v_ref, gs_ref, out_ref, c_ref, m_ref, *, chunk):
    L = chunk

    @pl.when(pl.program_id(1) == 0)
    def _():
        c_ref[...] = jnp.zeros_like(c_ref)
        m_ref[...] = jnp.zeros_like(m_ref)

    b, g, big_g = gs_ref[0:SUBLANES, :], gs_ref[SUBLANES:2 * SUBLANES, :], gs_ref[2 * SUBLANES:, :]
    m_prev = jnp.concatenate([m_ref[...]] * (L // LANES), axis=1)
    big_m = jnp.maximum(big_g, m_prev)
    inter_w = jnp.exp(m_prev - big_m)
    floor = jnp.exp(-(b + big_m))
    m_end = big_m[:, L - 1:L]
    w_state = jnp.exp(g - m_end)
    m_ref[...] = jnp.broadcast_to(b[:, L - 1:L] + m_end, m_ref.shape)
    stats = [big_m, inter_w, floor, w_state]
    pad = jnp.zeros((LANES - len(stats) * SUBLANES, L), F32)
    cols = jnp.concatenate(stats + [pad], axis=0).T

    row = lax.broadcasted_iota(jnp.int32, (L, L), 0)
    col = lax.broadcasted_iota(jnp.int32, (L, L), 1)
    causal = row >= col
    lane = lax.broadcasted_iota(jnp.int32, (L, LANES), 1)
    ones_blk = jnp.where(lane == 0, 1.0, 0.0).astype(BF16)

    for h in range(ML_HEADS):
        pair, lo = h // 2, (h % 2) * ML_QK
        blk = slice(pair * LANES, (pair + 1) * LANES)
        mine = (lane >= lo) & (lane < lo + ML_QK)
        qm = jnp.where(mine, q_ref[:, blk], jnp.zeros((), BF16))
        k_blk = k_ref[:, blk]
        s = _dot_nt(qm, k_blk)
        dw = jnp.where(causal, jnp.exp(g[h:h + 1, :] - cols[:, h:h + 1]), 0.0)
        v_ext = jnp.concatenate([v_ref[:, h * ML_V:(h + 1) * ML_V], ones_blk], axis=1)
        c_pair = c_ref[pair]
        iw = cols[:, SUBLANES + h:SUBLANES + h + 1]
        num = _dot((s * dw).astype(BF16), v_ext) + iw * _dot(qm, c_pair.astype(BF16))
        den = num[:, ML_V:ML_V + 1]
        hh = num[:, :ML_V] / jnp.maximum(jnp.abs(den), cols[:, 2 * SUBLANES + h:2 * SUBLANES + h + 1])
        out_ref[:, h * ML_V:(h + 1) * ML_V] = hh.astype(out_ref.dtype)

        kw = (k_blk.astype(F32) * cols[:, 3 * SUBLANES + h:3 * SUBLANES + h + 1]).astype(BF16)
        upd = _dot_tn(kw, v_ext)
        rows = slice(lo, lo + ML_QK)
        c_ref[pair, rows, :] = inter_w[h:h + 1, L - 1:L] * c_pair[rows, :] + upd[rows, :]


def _mlstm(q, k, v, gs, *, batch, seq, chunk):
    t = q.shape[0]
    nc = seq // chunk
    row = lambda b, c: (b * nc + c, 0)
    return pl.pallas_call(
        functools.partial(_mlstm_kernel, chunk=chunk),
        out_shape=jax.ShapeDtypeStruct((t, ML_WIDTH), BF16),
        grid=(batch, nc),
        in_specs=[pl.BlockSpec((chunk, ML_QK_WIDTH), row), pl.BlockSpec((chunk, ML_QK_WIDTH), row),
                  pl.BlockSpec((chunk, ML_WIDTH), row),
                  pl.BlockSpec((3 * SUBLANES, chunk), lambda b, c: (0, b * nc + c))],
        out_specs=pl.BlockSpec((chunk, ML_WIDTH), row),
        scratch_shapes=[pltpu.VMEM((ML_HEADS // 2, LANES, 2 * LANES), F32),
                        pltpu.VMEM((SUBLANES, LANES), F32)],
        compiler_params=_params("parallel", "arbitrary"), name="mlstm",
    )(q, k, v, gs)


def _attn_kernel(q_ref, k_ref, vt_ref, out_ref, s_sc, smax_sc, m_sc, acc_sc, *, tb, hp):
    last = pl.program_id(2)
    heads = range(hp)
    ones_rows = jnp.where(lax.broadcasted_iota(jnp.int32, (16, tb), 0) == 0, 1.0, 0.0).astype(BF16)

    def produce(j, slot, h, diagonal):
        kb = k_ref[pl.ds(pl.multiple_of(j * tb, tb), tb), h * HEAD_BLOCK:(h + 1) * HEAD_BLOCK]
        s = _dot_nt(kb, q_ref[:, h * HEAD_BLOCK:(h + 1) * HEAD_BLOCK])
        if diagonal:
            key = lax.broadcasted_iota(jnp.int32, s.shape, 0)
            qry = lax.broadcasted_iota(jnp.int32, s.shape, 1)
            s = jnp.where(key <= qry, s, -jnp.inf)
        s_sc[slot, h] = s
        smax_sc[slot, h] = jnp.max(s, axis=0, keepdims=True)

    def consume(j, slot, h):
        m = m_sc[h]
        m_new = jnp.maximum(m, smax_sc[slot, h])
        p = jnp.exp2(s_sc[slot, h] - m_new).astype(BF16)
        v_ext = jnp.concatenate([vt_ref[j, h * MLA_V:(h + 1) * MLA_V, :], ones_rows], axis=0)
        acc_sc[h] = jnp.exp2(m - m_new) * acc_sc[h] + _dot(v_ext, p)
        m_sc[h] = m_new

    def stage(j, slot, next_is_diagonal):
        for h in heads:
            produce(j + 1, 1 - slot, h, next_is_diagonal)
            consume(j, slot, h)

    def finish(slot):
        for h in heads:
            consume(last, slot, h)
        o_t = jnp.concatenate([acc_sc[h, :MLA_V, :] / acc_sc[h, MLA_V:MLA_V + 1, :] for h in heads], axis=0)
        out_ref[...] = o_t.T.astype(out_ref.dtype)

    m_sc[...] = jnp.full(m_sc.shape, -jnp.inf, F32)
    acc_sc[...] = jnp.zeros(acc_sc.shape, F32)

    @pl.when(last == 0)
    def _():
        for h in heads:
            produce(0, 0, h, True)
        finish(0)

    @pl.when(last > 0)
    def _():
        for h in heads:
            produce(0, 0, h, False)

        def four_stages(t, carry):
            for u in range(4):
                stage(4 * t + u, u % 2, False)
            return carry

        def two_stages(t, carry):
            stage(2 * t, 0, False)
            stage(2 * t + 1, 1, False)
            return carry

        quads = (last - 1) // 4
        lax.fori_loop(0, quads, four_stages, 0)
        lax.fori_loop(2 * quads, (last - 1) // 2, two_stages, 0)

        @pl.when(last % 2 == 1)
        def _():
            stage(last - 1, 0, True)
            finish(1)

        @pl.when(last % 2 == 0)
        def _():
            stage(last - 2, 0, False)
            stage(last - 1, 1, True)
            finish(0)


def _attention(q, k, vt, *, batch, seq, tb, hp):
    t = q.shape[0]
    nb = seq // tb
    return pl.pallas_call(
        functools.partial(_attn_kernel, tb=tb, hp=hp),
        out_shape=jax.ShapeDtypeStruct((t, MLA_WIDTH), BF16),
        grid=(batch, MLA_HEADS // hp, nb),
        in_specs=[pl.BlockSpec((tb, hp * HEAD_BLOCK), lambda b, p, i: (b * nb + i, p)),
                  pl.BlockSpec((seq, hp * HEAD_BLOCK), lambda b, p, i: (b, p)),
                  pl.BlockSpec((None, nb, hp * MLA_V, tb), lambda b, p, i: (b, 0, p, 0))],
        out_specs=pl.BlockSpec((tb, hp * MLA_V), lambda b, p, i: (b * nb + i, p)),
        scratch_shapes=[pltpu.VMEM((2, hp, tb, tb), F32), pltpu.VMEM((2, hp, 1, tb), F32),
                        pltpu.VMEM((hp, 1, tb), F32), pltpu.VMEM((hp, MLA_V + 16, tb), F32)],
        compiler_params=_params("parallel", "parallel", "arbitrary"), name="attn",
    )(q, k, vt)


def _mix_ffn_kernel(hml_ref, o_ref, hmla_ref, x_ref, hn_ref, wa_ref, wb_ref, gpost_ref, gpre_ref,
                    wu_ref, wd_ref, gmlp_ref, out_ref, m_sc, acc_sc, *, sub):
    j = pl.program_id(1)

    @pl.when(j == 0)
    def _():
        blocks = [slice(r, r + sub) for r in range(0, out_ref.shape[0], sub)]
        mixes = []
        for rows in blocks:
            heads = []
            for h in range(ML_HEADS):
                hs = slice(h * ML_V, (h + 1) * ML_V)
                hh = _rms(hml_ref[rows, hs].astype(F32), hn_ref[:, hs]) * jax.nn.sigmoid(o_ref[rows, hs].astype(F32))
                heads.append(hh.astype(BF16))
            mixes.append(_dot(jnp.concatenate(heads, axis=1), wa_ref[...]) + _dot(hmla_ref[rows, :], wb_ref[...]))
        for rows, mix in zip(blocks, mixes):
            x1 = x_ref[rows, :] + _rms(mix, gpost_ref[...])
            out_ref[rows, :] = x1
            m_sc[rows, :] = _rms(x1, gpre_ref[...]).astype(m_sc.dtype)

    h = jnp.square(jnp.maximum(_dot(m_sc[...], wu_ref[...]), 0.0)).astype(BF16)
    part = _dot(h, wd_ref[...])

    @pl.when(j == 0)
    def _():
        acc_sc[...] = part

    @pl.when(j > 0)
    def _():
        acc_sc[...] += part

    @pl.when(j == pl.num_programs(1) - 1)
    def _():
        out_ref[...] += _rms(acc_sc[...], gmlp_ref[...])


def _mix_ffn(hml, o, hmla, x2, params, layer, *, tm, tf):
    hn, wo, gpost, gpre, wu, wd, gmlp = params
    t, d = x2.shape
    dff = wu.shape[2]
    row = lambda i, j: (i, 0)
    vec = lambda arr: pl.BlockSpec((None,) + arr.shape[1:], lambda i, j: (layer, 0, 0))
    return pl.pallas_call(
        functools.partial(_mix_ffn_kernel, sub=min(tm, 256)),
        out_shape=jax.ShapeDtypeStruct((t, d), F32),
        grid=(t // tm, dff // tf),
        in_specs=[pl.BlockSpec((tm, ML_WIDTH), row), pl.BlockSpec((tm, ML_WIDTH), row),
                  pl.BlockSpec((tm, MLA_WIDTH), row), pl.BlockSpec((tm, d), row),
                  vec(hn),
                  pl.BlockSpec((None, ML_WIDTH, d), lambda i, j: (layer, 0, 0)),
                  pl.BlockSpec((None, MLA_WIDTH, d), lambda i, j: (layer, 1, 0)),
                  vec(gpost), vec(gpre),
                  pl.BlockSpec((None, d, tf), lambda i, j: (layer, 0, j)),
                  pl.BlockSpec((None, tf, d), lambda i, j: (layer, j, 0)), vec(gmlp)],
        out_specs=pl.BlockSpec((tm, d), row),
        scratch_shapes=[pltpu.VMEM((tm, d), BF16), pltpu.VMEM((tm, d), F32)],
        compiler_params=_params("parallel", "arbitrary"), name="mix_ffn",
    )(hml, o, hmla, x2, hn, wo, wo, gpost, gpre, wu, wd, gmlp)


def _layout_in_proj(w_in, b_gates):
    depth, d, _ = w_in.shape
    o0 = QK_COLS
    o1 = o0 + ML_WIDTH
    o2 = o1 + ML_WIDTH
    o3 = o2 + 2 * ML_HEADS
    o4 = o3 + MLA_Q_RANK
    o5 = o4 + MLA_KV_RANK
    gates, k_r = w_in[..., o2:o3], w_in[..., o5:]
    lead = jnp.zeros((depth, d, ROPE_LANE0), w_in.dtype)
    pad = jnp.zeros((depth, d, LANES - ROPE_LANE1), w_in.dtype)
    blka = jnp.concatenate([lead, k_r, pad], axis=2)
    blkb = jnp.concatenate([lead, -k_r[..., HALF_ROPE:], k_r[..., :HALF_ROPE], pad], axis=2)
    win = jnp.concatenate([w_in[..., :o2], w_in[..., o3:o5], blka, blkb], axis=2).astype(BF16)
    gates_t = jnp.swapaxes(gates, 1, 2)
    zrows = jnp.zeros((depth, SUBLANES - ML_HEADS, d), w_in.dtype)
    wg = jnp.concatenate([gates_t[:, :ML_HEADS], zrows, gates_t[:, ML_HEADS:], zrows], axis=1).astype(BF16)
    zb = jnp.zeros((depth, SUBLANES - ML_HEADS), b_gates.dtype)
    gbias = jnp.concatenate([b_gates[:, :ML_HEADS], zb, b_gates[:, ML_HEADS:], zb], axis=1)[..., None]
    return win, wg, gbias


def _layout_mla(w_uq, w_ukv):
    depth, rq, _ = w_uq.shape
    rkv = w_ukv.shape[1]
    wq4 = w_uq.reshape(depth, rq, MLA_HEADS, MLA_NOPE + MLA_ROPE)
    nope, rope = wq4[..., :MLA_NOPE], wq4[..., MLA_NOPE:]
    pad = jnp.zeros((depth, rq, MLA_HEADS, HEAD_BLOCK - ROPE_LANE1), w_uq.dtype)
    lead = jnp.zeros((depth, rq, MLA_HEADS, ROPE_LANE0), w_uq.dtype)
    wq = jnp.concatenate([nope, rope, pad], axis=3).reshape(depth, rq, MLA_HEADS * HEAD_BLOCK).astype(BF16)
    wqs = jnp.concatenate([lead, -rope[..., HALF_ROPE:], rope[..., :HALF_ROPE], pad], axis=3)
    wqs = wqs.reshape(depth, rq, MLA_HEADS * HEAD_BLOCK).astype(BF16)
    wkv4 = w_ukv.reshape(depth, rkv, MLA_HEADS, MLA_NOPE + MLA_V)
    kpad = jnp.zeros((depth, rkv, MLA_HEADS, HEAD_BLOCK - MLA_NOPE), w_ukv.dtype)
    wk = jnp.concatenate([wkv4[..., :MLA_NOPE], kpad], axis=3).reshape(depth, rkv, MLA_HEADS * HEAD_BLOCK).astype(BF16)
    wvt = jnp.swapaxes(wkv4[..., MLA_NOPE:].reshape(depth, rkv, MLA_WIDTH), 1, 2).astype(BF16)
    return wq, wqs, wk, wvt


def kernel(x, positions, norm_pre_mix, w_in, b_gates, conv_w, conv_b, ml_head_norm, q_norm, w_uq, kv_norm, w_ukv,
           w_out, norm_post_mix, norm_pre_mlp, w_up, w_down, norm_post_mlp):
    batch, seq, d = x.shape
    depth = w_in.shape[0]
    t = batch * seq
    tm_in = min(1024, seq)
    tb = min(256, seq)
    hp = 8
    chunk = 256
    tm_ffn = min(1024, t)
    tf = 1024

    rows = lambda v: v[:, None, :]
    win, wg, gbias = _layout_in_proj(w_in, b_gates)
    wq, wqs, wk, wvt = _layout_mla(w_uq, w_ukv)
    in_params = (rows(norm_pre_mix), win, wg, gbias, conv_w, rows(conv_b), rows(q_norm), wq, wqs, rows(kv_norm), wk, wvt)
    mix_params = (rows(ml_head_norm), w_out.astype(BF16), rows(norm_post_mix), rows(norm_pre_mlp),
                  w_up.astype(BF16), w_down.astype(BF16), rows(norm_post_mlp))

    cos_t, sin_t = _rope_tables(positions)
    x2 = x.reshape(t, d)
    for l in range(depth):
        qml, kml, v, o, gs, q, k, vt = _in_proj(x2, cos_t, sin_t, in_params, l,
                                                batch=batch, seq=seq, tm=tm_in, tk=tb, chunk=chunk)
        hml = _mlstm(qml, kml, v, gs, batch=batch, seq=seq, chunk=chunk)
        hmla = _attention(q, k, vt, batch=batch, seq=seq, tb=tb, hp=hp)
        x2 = _mix_ffn(hml, o, hmla, x2, mix_params, l, tm=tm_ffn, tf=tf)
    return x2.reshape(batch, seq, d)
```

```python
import functools

import jax
import jax.numpy as jnp
from jax import lax
from jax.experimental import pallas as pl
from jax.experimental.pallas import tpu as pltpu

F32 = jnp.float32
BF16 = jnp.bfloat16

EPS = 1e-6
ML_HEADS, ML_QK, ML_V, ML_CONV = 4, 64, 128, 4
MLA_HEADS, MLA_NOPE, MLA_ROPE, MLA_V = 8, 64, 32, 64
MLA_Q_RANK, MLA_KV_RANK = 256, 128
ROPE_THETA = 10000.0
LOG2_E = 1.4426950408889634

ML_WIDTH = ML_HEADS * ML_V
ML_QK_WIDTH = ML_HEADS * ML_QK
QK_COLS = 2 * ML_QK_WIDTH
MLA_WIDTH = MLA_HEADS * MLA_V

LANES = 128
SUBLANES = 8
HEAD_BLOCK = LANES
ROPE_LANE0 = MLA_NOPE
ROPE_LANE1 = MLA_NOPE + MLA_ROPE
HALF_ROPE = MLA_ROPE // 2
C_QK, C_V, C_O = 0, QK_COLS, QK_COLS + ML_WIDTH
C_CQ = C_O + ML_WIDTH
C_CKV = C_CQ + MLA_Q_RANK
C_KR = C_CKV + MLA_KV_RANK
IN_COLS = C_KR + LANES
CONV_HALO = SUBLANES

VMEM_LIMIT = 56 * 1024 * 1024


def _params(*sem):
    return pltpu.CompilerParams(dimension_semantics=sem, vmem_limit_bytes=VMEM_LIMIT)


def _rms(x, g):
    return x * lax.rsqrt(jnp.mean(x * x, axis=-1, keepdims=True) + EPS) * g


def _log_sigmoid(x):
    return -(jnp.maximum(-x, 0.0) + jnp.log1p(jnp.exp(-jnp.abs(x))))


def _dot(a, b):
    return jnp.dot(a, b, preferred_element_type=F32)


def _dot_nt(a, b):
    return lax.dot_general(a, b, (((1,), (1,)), ((), ())), preferred_element_type=F32)


def _dot_tn(a, b):
    return lax.dot_general(a, b, (((0,), (0,)), ((), ())), preferred_element_type=F32)


def _rope_table_kernel(pos_ref, inv_ref, cos_ref, sin_ref):
    pos = pos_ref[...].astype(F32)
    slot = lax.broadcasted_iota(jnp.int32, cos_ref.shape, 1) // HALF_ROPE
    ang = jnp.zeros(cos_ref.shape, F32)
    for t in range(pos.shape[1]):
        ang = jnp.where(slot == t, pos[:, t:t + 1], ang)
    ang = ang * inv_ref[...]
    cos_ref[...] = jnp.cos(ang)
    sin_ref[...] = jnp.sin(ang)


def _rope_tables(positions):
    t = positions.size
    per_row = LANES // HALF_ROPE
    rows = t // per_row
    tr = min(rows, 512)
    inv = 1.0 / (ROPE_THETA ** (jnp.arange(0, MLA_ROPE, 2, dtype=F32) / MLA_ROPE))
    cos_c, sin_c = pl.pallas_call(
        _rope_table_kernel,
        out_shape=(jax.ShapeDtypeStruct((rows, LANES), F32),) * 2,
        grid=(rows // tr,),
        in_specs=[pl.BlockSpec((tr, per_row), lambda i: (i, 0)),
                  pl.BlockSpec((1, LANES), lambda i: (0, 0))],
        out_specs=(pl.BlockSpec((tr, LANES), lambda i: (i, 0)),) * 2,
        compiler_params=_params("parallel"),
        name="rope_tables",
    )(positions.reshape(rows, per_row), jnp.tile(inv, per_row).reshape(1, LANES))
    cos_h, sin_h = cos_c.reshape(t, HALF_ROPE), sin_c.reshape(t, HALF_ROPE)
    ones = jnp.ones((t, ROPE_LANE0), F32)
    zeros = jnp.zeros((t, ROPE_LANE0), F32)
    pad = jnp.zeros((t, LANES - ROPE_LANE1), F32)
    return (jnp.concatenate([ones, cos_h, cos_h, pad], axis=1), jnp.concatenate([zeros, sin_h, sin_h, pad], axis=1))


def _scan_lanes(x, op, identity, segment):
    pos = lax.broadcasted_iota(jnp.int32, x.shape, 1) % segment
    step = 1
    while step < segment:
        x = op(x, jnp.where(pos >= step, pltpu.roll(x, step, axis=1), identity))
        step *= 2
    return x


def _in_proj_kernel(x_ref, cos_ref, sin_ref, g_ref, win_ref, wg_ref, gbias_ref, cw_ref, cb_ref, qn_ref, wq_ref,
                    kvn_ref, wk_ref, wvt_ref,
                    qml_ref, kml_ref, v_ref, o_ref, gs_ref, q_ref, k_ref, vt_ref, xbuf,
                    *, tiles_per_batch, chunk, sub):
    tm = x_ref.shape[0]
    scale = (MLA_NOPE + MLA_ROPE) ** -0.5 * LOG2_E

    @pl.when(pl.program_id(0) % tiles_per_batch == 0)
    def _():
        xbuf[0:CONV_HALO, :] = jnp.zeros((CONV_HALO, QK_COLS), F32)

    blocks = [slice(r, r + sub) for r in range(0, tm, sub)]
    acts = [_rms(x_ref[rows, :], g_ref[...]).astype(BF16) for rows in blocks]
    projs = [_dot(a, win_ref[...]) for a in acts]

    gates = jnp.concatenate([_dot_nt(wg_ref[...], a) for a in acts], axis=1) + gbias_ref[...]
    b = _scan_lanes(_log_sigmoid(gates[SUBLANES:]), jnp.add, 0.0, chunk)
    g = gates[:SUBLANES] - b
    gs_ref[...] = jnp.concatenate([b, g, _scan_lanes(g, jnp.maximum, -jnp.inf, chunk)], axis=0)

    for rows, a, proj in zip(blocks, acts, projs):
        v_ref[rows, :] = proj[:, C_V:C_O].astype(BF16)
        o_ref[rows, :] = proj[:, C_O:C_CQ].astype(BF16)

        top = CONV_HALO + rows.start
        xbuf[top:top + sub, :] = proj[:, C_QK:C_V]
        cw = cw_ref[...]
        y = cb_ref[...]
        for tap in range(ML_CONV):
            start = top - (ML_CONV - 1 - tap)
            y = y + xbuf[start:start + sub, :] * cw[tap:tap + 1, :]
        qk = y * jax.nn.sigmoid(y)
        qml_ref[rows, :] = qk[:, :ML_QK_WIDTH].astype(BF16)
        kml_ref[rows, :] = (qk[:, ML_QK_WIDTH:] * (ML_QK ** -0.5)).astype(BF16)

        cos_t = cos_ref[rows, :]
        lane = lax.broadcasted_iota(jnp.int32, cos_t.shape, 1)
        first_half = lane < ROPE_LANE0 + HALF_ROPE
        sin_t = sin_ref[rows, :]
        sin_q = jnp.where(first_half, -sin_t, sin_t)
        qn = _rms(proj[:, C_CQ:C_CKV], qn_ref[...]).astype(BF16)
        qf = _dot(qn, wq_ref[...])
        ckvn = _rms(proj[:, C_CKV:C_KR], kvn_ref[...]).astype(BF16)
        kf = _dot(ckvn, wk_ref[...])
        kr = proj[:, C_KR:IN_COLS]
        rope = (lane >= ROPE_LANE0) & (lane < ROPE_LANE1)
        k_rope = jnp.where(rope, kr * cos_t + pltpu.roll(kr, LANES // 2, axis=1) * sin_t, 0.0)
        for h in range(MLA_HEADS):
            blk = slice(h * HEAD_BLOCK, (h + 1) * HEAD_BLOCK)
            qh = qf[:, blk]
            swapped = pltpu.roll(qh, LANES - HALF_ROPE, axis=1)
            q_ref[rows, blk] = ((qh * cos_t + swapped * sin_q) * scale).astype(BF16)
            k_ref[rows, blk] = (kf[:, blk] + k_rope).astype(BF16)

        vt = _dot_nt(wvt_ref[...], ckvn).astype(BF16)
        if len(vt_ref.shape) == 2:
            vt_ref[:, rows] = vt
        else:
            tk = vt_ref.shape[-1]
            for j in range(sub // tk):
                vt_ref[rows.start // tk + j] = vt[:, j * tk:(j + 1) * tk]

    xbuf[0:CONV_HALO, :] = xbuf[tm:tm + CONV_HALO, :]


def _in_proj(x2, cos_t, sin_t, params, layer, *, batch, seq, tm, tk, chunk):
    t, d = x2.shape
    nt = t // tm
    tiles_per_batch = seq // tm
    if tk >= tm:
        per_kv = tk // tm
        vt_spec = pl.BlockSpec((None, None, MLA_WIDTH, tm),
                               lambda i: (i // tiles_per_batch, (i % tiles_per_batch) // per_kv, 0, i % per_kv))
    else:
        vt_spec = pl.BlockSpec((None, tm // tk, MLA_WIDTH, tk),
                               lambda i: (i // tiles_per_batch, i % tiles_per_batch, 0, 0))
    row = lambda i: (i, 0)
    col = lambda i: (0, i)
    of_layer = lambda arr: pl.BlockSpec((None,) + arr.shape[1:], lambda i: (layer, 0, 0))
    out_shape = (
        jax.ShapeDtypeStruct((t, ML_QK_WIDTH), BF16),
        jax.ShapeDtypeStruct((t, ML_QK_WIDTH), BF16),
        jax.ShapeDtypeStruct((t, ML_WIDTH), BF16),
        jax.ShapeDtypeStruct((t, ML_WIDTH), BF16),
        jax.ShapeDtypeStruct((3 * SUBLANES, t), F32),
        jax.ShapeDtypeStruct((t, MLA_HEADS * HEAD_BLOCK), BF16),
        jax.ShapeDtypeStruct((t, MLA_HEADS * HEAD_BLOCK), BF16),
        jax.ShapeDtypeStruct((batch, seq // tk, MLA_WIDTH, tk), BF16),
    )
    out_specs = (
        pl.BlockSpec((tm, ML_QK_WIDTH), row),
        pl.BlockSpec((tm, ML_QK_WIDTH), row),
        pl.BlockSpec((tm, ML_WIDTH), row),
        pl.BlockSpec((tm, ML_WIDTH), row),
        pl.BlockSpec((3 * SUBLANES, tm), col),
        pl.BlockSpec((tm, MLA_HEADS * HEAD_BLOCK), row),
        pl.BlockSpec((tm, MLA_HEADS * HEAD_BLOCK), row),
        vt_spec,
    )
    in_specs = [pl.BlockSpec((tm, d), row), pl.BlockSpec((tm, LANES), row), pl.BlockSpec((tm, LANES), row)]
    in_specs += [of_layer(p) for p in params]
    return pl.pallas_call(
        functools.partial(_in_proj_kernel, tiles_per_batch=tiles_per_batch, chunk=chunk, sub=min(tm, 256)),
        out_shape=out_shape, grid=(nt,), in_specs=in_specs, out_specs=out_specs,
        scratch_shapes=[pltpu.VMEM((tm + CONV_HALO, QK_COLS), F32)],
        compiler_params=_params("arbitrary"), name="in_proj",
    )(x2, cos_t, sin_t, *params)


def _mlstm_kernel(q_ref, k_ref, v_ref, gs_ref, out_ref, c_ref, m_ref, *, chunk):
    L = chunk

    @pl.when(pl.program_id(1) == 0)
    def _():
        c_ref[...] = jnp.zeros_like(c_ref)
        m_ref[...] = jnp.zeros_like(m_ref)

    b, g, big_g = gs_ref[0:SUBLANES, :], gs_ref[SUBLANES:2 * SUBLANES, :], gs_ref[2 * SUBLANES:, :]
    m_prev = jnp.concatenate([m_ref[...]] * (L // LANES), axis=1)
    big_m = jnp.maximum(big_g, m_prev)
    inter_w = jnp.exp(m_prev - big_m)
    floor = jnp.exp(-(b + big_m))
    m_end = big_m[:, L - 1:L]
    w_state = jnp.exp(g - m_end)
    m_ref[...] = jnp.broadcast_to(b[:, L - 1:L] + m_end, m_ref.shape)
    stats = [big_m, inter_w, floor, w_state]
    pad = jnp.zeros((LANES - len(stats) * SUBLANES, L), F32)
    cols = jnp.concatenate(stats + [pad], axis=0).T

    row = lax.broadcasted_iota(jnp.int32, (L, L), 0)
    col = lax.broadcasted_iota(jnp.int32, (L, L), 1)
    causal = row >= col
    lane = lax.broadcasted_iota(jnp.int32, (L, LANES), 1)
    ones_blk = jnp.where(lane == 0, 1.0, 0.0).astype(BF16)

    for h in range(ML_HEADS):
        pair, lo = h // 2, (h % 2) * ML_QK
        blk = slice(pair * LANES, (pair + 1) * LANES)
        mine = (lane >= lo) & (lane < lo + ML_QK)
        qm = jnp.where(mine, q_ref[:, blk], jnp.zeros((), BF16))
        k_blk = k_ref[:, blk]
        s = _dot_nt(qm, k_blk)
        dw = jnp.where(causal, jnp.exp(g[h:h + 1, :] - cols[:, h:h + 1]), 0.0)
        v_ext = jnp.concatenate([v_ref[:, h * ML_V:(h + 1) * ML_V], ones_blk], axis=1)
        c_pair = c_ref[pair]
        iw = cols[:, SUBLANES + h:SUBLANES + h + 1]
        num = _dot((s * dw).astype(BF16), v_ext) + iw * _dot(qm, c_pair.astype(BF16))
        den = num[:, ML_V:ML_V + 1]
        hh = num[:, :ML_V] / jnp.maximum(jnp.abs(den), cols[:, 2 * SUBLANES + h:2 * SUBLANES + h + 1])
        out_ref[:, h * ML_V:(h + 1) * ML_V] = hh.astype(out_ref.dtype)

        kw = (k_blk.astype(F32) * cols[:, 3 * SUBLANES + h:3 * SUBLANES + h + 1]).astype(BF16)
        upd = _dot_tn(kw, v_ext)
        rows = slice(lo, lo + ML_QK)
        c_ref[pair, rows, :] = inter_w[h:h + 1, L - 1:L] * c_pair[rows, :] + upd[rows, :]


def _mlstm(q, k, v, gs, *, batch, seq, chunk):
    t = q.shape[0]
    nc = seq // chunk
    row = lambda b, c: (b * nc + c, 0)
    return pl.pallas_call(
        functools.partial(_mlstm_kernel, chunk=chunk),
        out_shape=jax.ShapeDtypeStruct((t, ML_WIDTH), BF16),
        grid=(batch, nc),
        in_specs=[pl.BlockSpec((chunk, ML_QK_WIDTH), row), pl.BlockSpec((chunk, ML_QK_WIDTH), row),
                  pl.BlockSpec((chunk, ML_WIDTH), row),
                  pl.BlockSpec((3 * SUBLANES, chunk), lambda b, c: (0, b * nc + c))],
        out_specs=pl.BlockSpec((chunk, ML_WIDTH), row),
        scratch_shapes=[pltpu.VMEM((ML_HEADS // 2, LANES, 2 * LANES), F32),
                        pltpu.VMEM((SUBLANES, LANES), F32)],
        compiler_params=_params("parallel", "arbitrary"), name="mlstm",
    )(q, k, v, gs)


def _attn_kernel(q_ref, k_ref, vt_ref, out_ref, s_sc, smax_sc, m_sc, acc_sc, *, tb, hp):
    last = pl.program_id(2)
    heads = range(hp)
    ones_rows = jnp.where(lax.broadcasted_iota(jnp.int32, (16, tb), 0) == 0, 1.0, 0.0).astype(BF16)

    def produce(j, slot, h, diagonal):
        kb = k_ref[pl.ds(pl.multiple_of(j * tb, tb), tb), h * HEAD_BLOCK:(h + 1) * HEAD_BLOCK]
        s = _dot_nt(kb, q_ref[:, h * HEAD_BLOCK:(h + 1) * HEAD_BLOCK])
        if diagonal:
            key = lax.broadcasted_iota(jnp.int32, s.shape, 0)
            qry = lax.broadcasted_iota(jnp.int32, s.shape, 1)
            s = jnp.where(key <= qry, s, -jnp.inf)
        s_sc[slot, h] = s
        smax_sc[slot, h] = jnp.max(s, axis=0, keepdims=True)

    def consume(j, slot, h):
        m = m_sc[h]
        m_new = jnp.maximum(m, smax_sc[slot, h])
        p = jnp.exp2(s_sc[slot, h] - m_new).astype(BF16)
        v_ext = jnp.concatenate([vt_ref[j, h * MLA_V:(h + 1) * MLA_V, :], ones_rows], axis=0)
        acc_sc[h] = jnp.exp2(m - m_new) * acc_sc[h] + _dot(v_ext, p)
        m_sc[h] = m_new

    def stage(j, slot, next_is_diagonal):
        for h in heads:
            produce(j + 1, 1 - slot, h, next_is_diagonal)
            consume(j, slot, h)

    def finish(slot):
        for h in heads:
            consume(last, slot, h)
        o_t = jnp.concatenate([acc_sc[h, :MLA_V, :] / acc_sc[h, MLA_V:MLA_V + 1, :] for h in heads], axis=0)
        out_ref[...] = o_t.T.astype(out_ref.dtype)

    m_sc[...] = jnp.full(m_sc.shape, -jnp.inf, F32)
    acc_sc[...] = jnp.zeros(acc_sc.shape, F32)

    @pl.when(last == 0)
    def _():
        for h in heads:
            produce(0, 0, h, True)
        finish(0)

    @pl.when(last > 0)
    def _():
        for h in heads:
            produce(0, 0, h, False)

        def four_stages(t, carry):
            for u in range(4):
                stage(4 * t + u, u % 2, False)
            return carry

        def two_stages(t, carry):
            stage(2 * t, 0, False)
            stage(2 * t + 1, 1, False)
            return carry

        quads = (last - 1) // 4
        lax.fori_loop(0, quads, four_stages, 0)
        lax.fori_loop(2 * quads, (last - 1) // 2, two_stages, 0)

        @pl.when(last % 2 == 1)
        def _():
            stage(last - 1, 0, True)
            finish(1)

        @pl.when(last % 2 == 0)
        def _():
            stage(last - 2, 0, False)
            stage(last - 1, 1, True)
            finish(0)


def _attention(q, k, vt, *, batch, seq, tb, hp):
    t = q.shape[0]
    nb = seq // tb
    return pl.pallas_call(
        functools.partial(_attn_kernel, tb=tb, hp=hp),
        out_shape=jax.ShapeDtypeStruct((t, MLA_WIDTH), BF16),
        grid=(batch, MLA_HEADS // hp, nb),
        in_specs=[pl.BlockSpec((tb, hp * HEAD_BLOCK), lambda b, p, i: (b * nb + i, p)),
                  pl.BlockSpec((seq, hp * HEAD_BLOCK), lambda b, p, i: (b, p)),
                  pl.BlockSpec((None, nb, hp * MLA_V, tb), lambda b, p, i: (b, 0, p, 0))],
        out_specs=pl.BlockSpec((tb, hp * MLA_V), lambda b, p, i: (b * nb + i, p)),
        scratch_shapes=[pltpu.VMEM((2, hp, tb, tb), F32), pltpu.VMEM((2, hp, 1, tb), F32),
                        pltpu.VMEM((hp, 1, tb), F32), pltpu.VMEM((hp, MLA_V + 16, tb), F32)],
        compiler_params=_params("parallel", "parallel", "arbitrary"), name="attn",
    )(q, k, vt)


def _mix_ffn_kernel(hml_ref, o_ref, hmla_ref, x_ref, hn_ref, wa_ref, wb_ref, gpost_ref, gpre_ref,
                    wu_ref, wd_ref, gmlp_ref, out_ref, m_sc, acc_sc, *, sub):
    j = pl.program_id(1)

    @pl.when(j == 0)
    def _():
        blocks = [slice(r, r + sub) for r in range(0, out_ref.shape[0], sub)]
        mixes = []
        for rows in blocks:
            heads = []
            for h in range(ML_HEADS):
                hs = slice(h * ML_V, (h + 1) * ML_V)
                hh = _rms(hml_ref[rows, hs].astype(F32), hn_ref[:, hs]) * jax.nn.sigmoid(o_ref[rows, hs].astype(F32))
                heads.append(hh.astype(BF16))
            mixes.append(_dot(jnp.concatenate(heads, axis=1), wa_ref[...]) + _dot(hmla_ref[rows, :], wb_ref[...]))
        for rows, mix in zip(blocks, mixes):
            x1 = x_ref[rows, :] + _rms(mix, gpost_ref[...])
            out_ref[rows, :] = x1
            m_sc[rows, :] = _rms(x1, gpre_ref[...]).astype(m_sc.dtype)
        acc_sc[...] = jnp.zeros(acc_sc.shape, F32)

    h = jnp.square(jnp.maximum(_dot(m_sc[...], wu_ref[...]), 0.0)).astype(BF16)
    acc_sc[...] += _dot(h, wd_ref[...])

    @pl.when(j == pl.num_programs(1) - 1)
    def _():
        out_ref[...] += _rms(acc_sc[...], gmlp_ref[...])


def _mix_ffn(hml, o, hmla, x2, params, layer, *, tm, tf):
    hn, wo, gpost, gpre, wu, wd, gmlp = params
    t, d = x2.shape
    dff = wu.shape[2]
    row = lambda i, j: (i, 0)
    vec = lambda arr: pl.BlockSpec((None,) + arr.shape[1:], lambda i, j: (layer, 0, 0))
    return pl.pallas_call(
        functools.partial(_mix_ffn_kernel, sub=min(tm, 256)),
        out_shape=jax.ShapeDtypeStruct((t, d), F32),
        grid=(t // tm, dff // tf),
        in_specs=[pl.BlockSpec((tm, ML_WIDTH), row), pl.BlockSpec((tm, ML_WIDTH), row),
                  pl.BlockSpec((tm, MLA_WIDTH), row), pl.BlockSpec((tm, d), row),
                  vec(hn),
                  pl.BlockSpec((None, ML_WIDTH, d), lambda i, j: (layer, 0, 0)),
                  pl.BlockSpec((None, MLA_WIDTH, d), lambda i, j: (layer, 1, 0)),
                  vec(gpost), vec(gpre),
                  pl.BlockSpec((None, d, tf), lambda i, j: (layer, 0, j)),
                  pl.BlockSpec((None, tf, d), lambda i, j: (layer, j, 0)), vec(gmlp)],
        out_specs=pl.BlockSpec((tm, d), row),
        scratch_shapes=[pltpu.VMEM((tm, d), BF16), pltpu.VMEM((tm, d), F32)],
        compiler_params=_params("parallel", "arbitrary"), name="mix_ffn",
    )(hml, o, hmla, x2, hn, wo, wo, gpost, gpre, wu, wd, gmlp)


def _layout_in_proj(w_in, b_gates):
    depth, d, _ = w_in.shape
    o0 = QK_COLS
    o1 = o0 + ML_WIDTH
    o2 = o1 + ML_WIDTH
    o3 = o2 + 2 * ML_HEADS
    o4 = o3 + MLA_Q_RANK
    o5 = o4 + MLA_KV_RANK
    gates, k_r = w_in[..., o2:o3], w_in[..., o5:]
    pad = jnp.zeros((depth, d, LANES - ROPE_LANE1), w_in.dtype)
    kr_blk = jnp.concatenate([-k_r[..., HALF_ROPE:], k_r[..., :HALF_ROPE], pad, k_r, pad], axis=2)
    win = jnp.concatenate([w_in[..., :o2], w_in[..., o3:o5], kr_blk], axis=2).astype(BF16)
    gates_t = jnp.swapaxes(gates, 1, 2)
    zrows = jnp.zeros((depth, SUBLANES - ML_HEADS, d), w_in.dtype)
    wg = jnp.concatenate([gates_t[:, :ML_HEADS], zrows, gates_t[:, ML_HEADS:], zrows], axis=1).astype(BF16)
    zb = jnp.zeros((depth, SUBLANES - ML_HEADS), b_gates.dtype)
    gbias = jnp.concatenate([b_gates[:, :ML_HEADS], zb, b_gates[:, ML_HEADS:], zb], axis=1)[..., None]
    return win, wg, gbias


def _layout_mla(w_uq, w_ukv):
    depth, rq, _ = w_uq.shape
    rkv = w_ukv.shape[1]
    wq4 = w_uq.reshape(depth, rq, MLA_HEADS, MLA_NOPE + MLA_ROPE)
    nope, rope = wq4[..., :MLA_NOPE], wq4[..., MLA_NOPE:]
    pad = jnp.zeros((depth, rq, MLA_HEADS, HEAD_BLOCK - ROPE_LANE1 - HALF_ROPE), w_uq.dtype)
    wq = jnp.concatenate([nope, rope, rope[..., :HALF_ROPE], pad], axis=3)
    wq = wq.reshape(depth, rq, MLA_HEADS * HEAD_BLOCK).astype(BF16)
    wkv4 = w_ukv.reshape(depth, rkv, MLA_HEADS, MLA_NOPE + MLA_V)
    kpad = jnp.zeros((depth, rkv, MLA_HEADS, HEAD_BLOCK - MLA_NOPE), w_ukv.dtype)
    wk = jnp.concatenate([wkv4[..., :MLA_NOPE], kpad], axis=3).reshape(depth, rkv, MLA_HEADS * HEAD_BLOCK).astype(BF16)
    wvt = jnp.swapaxes(wkv4[..., MLA_NOPE:].reshape(depth, rkv, MLA_WIDTH), 1, 2).astype(BF16)
    return wq, wk, wvt


def kernel(x, positions, norm_pre_mix, w_in, b_gates, conv_w, conv_b, ml_head_norm, q_norm, w_uq, kv_norm, w_ukv,
           w_out, norm_post_mix, norm_pre_mlp, w_up, w_down, norm_post_mlp):
    batch, seq, d = x.shape
    depth = w_in.shape[0]
    t = batch * seq
    tm_in = min(1024, seq)
    tb = min(256, seq)
    hp = 8
    chunk = 256
    tm_ffn = min(1024, t)
    tf = 1024

    rows = lambda v: v[:, None, :]
    win, wg, gbias = _layout_in_proj(w_in, b_gates)
    wq, wk, wvt = _layout_mla(w_uq, w_ukv)
    in_params = (rows(norm_pre_mix), win, wg, gbias, conv_w, rows(conv_b), rows(q_norm), wq, rows(kv_norm), wk, wvt)
    mix_params = (rows(ml_head_norm), w_out.astype(BF16), rows(norm_post_mix), rows(norm_pre_mlp),
                  w_up.astype(BF16), w_down.astype(BF16), rows(norm_post_mlp))

    cos_t, sin_t = _rope_tables(positions)
    x2 = x.reshape(t, d)
    for l in range(depth):
        qml, kml, v, o, gs, q, k, vt = _in_proj(x2, cos_t, sin_t, in_params, l,
                                                batch=batch, seq=seq, tm=tm_in, tk=tb, chunk=chunk)
        hml = _mlstm(qml, kml, v, gs, batch=batch, seq=seq, chunk=chunk)
        hmla = _attention(q, k, vt, batch=batch, seq=seq, tb=tb, hp=hp)
        x2 = _mix_ffn(hml, o, hmla, x2, mix_params, l, tm=tm_ffn, tf=tf)
    return x2.reshape(batch, seq, d)
```

```python
import functools

import jax
import jax.numpy as jnp
from jax import lax
from jax.experimental import pallas as pl
from jax.experimental.pallas import tpu as pltpu

F32 = jnp.float32
BF16 = jnp.bfloat16

EPS = 1e-6
ML_HEADS, ML_QK, ML_V, ML_CONV = 4, 64, 128, 4
MLA_HEADS, MLA_NOPE, MLA_ROPE, MLA_V = 8, 64, 32, 64
MLA_Q_RANK, MLA_KV_RANK = 256, 128
ROPE_THETA = 10000.0
LOG2_E = 1.4426950408889634

ML_WIDTH = ML_HEADS * ML_V
ML_QK_WIDTH = ML_HEADS * ML_QK
QK_COLS = 2 * ML_QK_WIDTH
MLA_WIDTH = MLA_HEADS * MLA_V

LANES = 128
SUBLANES = 8
HEAD_BLOCK = LANES
ROPE_LANE0 = MLA_NOPE
ROPE_LANE1 = MLA_NOPE + MLA_ROPE
HALF_ROPE = MLA_ROPE // 2
C_QK, C_V, C_O = 0, QK_COLS, QK_COLS + ML_WIDTH
C_CQ = C_O + ML_WIDTH
C_CKV = C_CQ + MLA_Q_RANK
C_KR = C_CKV + MLA_KV_RANK
IN_COLS = C_KR + LANES
CONV_HALO = SUBLANES

VMEM_LIMIT = 56 * 1024 * 1024


def _params(*sem):
    return pltpu.CompilerParams(dimension_semantics=sem, vmem_limit_bytes=VMEM_LIMIT)


def _rms(x, g):
    return x * lax.rsqrt(jnp.mean(x * x, axis=-1, keepdims=True) + EPS) * g


def _log_sigmoid(x):
    return -(jnp.maximum(-x, 0.0) + jnp.log1p(jnp.exp(-jnp.abs(x))))


def _dot(a, b):
    return jnp.dot(a, b, preferred_element_type=F32)


def _dot_nt(a, b):
    return lax.dot_general(a, b, (((1,), (1,)), ((), ())), preferred_element_type=F32)


def _dot_tn(a, b):
    return lax.dot_general(a, b, (((0,), (0,)), ((), ())), preferred_element_type=F32)


def _rope_table_kernel(pos_ref, inv_ref, cos_ref, sin_ref):
    pos = pos_ref[...].astype(F32)
    slot = lax.broadcasted_iota(jnp.int32, cos_ref.shape, 1) // HALF_ROPE
    ang = jnp.zeros(cos_ref.shape, F32)
    for t in range(pos.shape[1]):
        ang = jnp.where(slot == t, pos[:, t:t + 1], ang)
    ang = ang * inv_ref[...]
    cos_ref[...] = jnp.cos(ang)
    sin_ref[...] = jnp.sin(ang)


def _rope_tables(positions):
    t = positions.size
    per_row = LANES // HALF_ROPE
    rows = t // per_row
    tr = min(rows, 512)
    inv = 1.0 / (ROPE_THETA ** (jnp.arange(0, MLA_ROPE, 2, dtype=F32) / MLA_ROPE))
    cos_c, sin_c = pl.pallas_call(
        _rope_table_kernel,
        out_shape=(jax.ShapeDtypeStruct((rows, LANES), F32),) * 2,
        grid=(rows // tr,),
        in_specs=[pl.BlockSpec((tr, per_row), lambda i: (i, 0)),
                  pl.BlockSpec((1, LANES), lambda i: (0, 0))],
        out_specs=(pl.BlockSpec((tr, LANES), lambda i: (i, 0)),) * 2,
        compiler_params=_params("parallel"),
        name="rope_tables",
    )(positions.reshape(rows, per_row), jnp.tile(inv, per_row).reshape(1, LANES))
    cos_h, sin_h = cos_c.reshape(t, HALF_ROPE), sin_c.reshape(t, HALF_ROPE)
    ones = jnp.ones((t, ROPE_LANE0), F32)
    zeros = jnp.zeros((t, ROPE_LANE0), F32)
    pad = jnp.zeros((t, LANES - ROPE_LANE1), F32)
    return (jnp.concatenate([ones, cos_h, cos_h, pad], axis=1), jnp.concatenate([zeros, sin_h, sin_h, pad], axis=1))


def _scan_lanes(x, op, identity, segment):
    pos = lax.broadcasted_iota(jnp.int32, x.shape, 1) % segment
    step = 1
    while step < segment:
        x = op(x, jnp.where(pos >= step, pltpu.roll(x, step, axis=1), identity))
        step *= 2
    return x


def _in_proj_kernel(x_ref, cos_ref, sin_ref, g_ref, win_ref, wg_ref, gbias_ref, cw_ref, cb_ref, qn_ref, wq_ref,
                    kvn_ref, wk_ref, wvt_ref,
                    qml_ref, kml_ref, v_ref, o_ref, gs_ref, q_ref, k_ref, vt_ref, xbuf,
                    *, tiles_per_batch, chunk, sub):
    tm = x_ref.shape[0]
    scale = (MLA_NOPE + MLA_ROPE) ** -0.5 * LOG2_E

    @pl.when(pl.program_id(0) % tiles_per_batch == 0)
    def _():
        xbuf[0:CONV_HALO, :] = jnp.zeros((CONV_HALO, QK_COLS), F32)

    blocks = [slice(r, r + sub) for r in range(0, tm, sub)]
    acts = [_rms(x_ref[rows, :], g_ref[...]).astype(BF16) for rows in blocks]
    projs = [_dot(a, win_ref[...]) for a in acts]

    gates = jnp.concatenate([_dot_nt(wg_ref[...], a) for a in acts], axis=1) + gbias_ref[...]
    b = _scan_lanes(_log_sigmoid(gates[SUBLANES:]), jnp.add, 0.0, chunk)
    g = gates[:SUBLANES] - b
    gs_ref[...] = jnp.concatenate([b, g, _scan_lanes(g, jnp.maximum, -jnp.inf, chunk)], axis=0)

    for rows, a, proj in zip(blocks, acts, projs):
        v_ref[rows, :] = proj[:, C_V:C_O].astype(BF16)
        o_ref[rows, :] = proj[:, C_O:C_CQ].astype(BF16)

        top = CONV_HALO + rows.start
        xbuf[top:top + sub, :] = proj[:, C_QK:C_V]
        cw = cw_ref[...]
        y = cb_ref[...]
        for tap in range(ML_CONV):
            start = top - (ML_CONV - 1 - tap)
            y = y + xbuf[start:start + sub, :] * cw[tap:tap + 1, :]
        qk = y * jax.nn.sigmoid(y)
        qml_ref[rows, :] = qk[:, :ML_QK_WIDTH].astype(BF16)
        kml_ref[rows, :] = (qk[:, ML_QK_WIDTH:] * (ML_QK ** -0.5)).astype(BF16)

        cos_t = cos_ref[rows, :]
        lane = lax.broadcasted_iota(jnp.int32, cos_t.shape, 1)
        first_half = lane < ROPE_LANE0 + HALF_ROPE
        sin_t = sin_ref[rows, :]
        sin_q = jnp.where(first_half, -sin_t, sin_t)
        qn = _rms(proj[:, C_CQ:C_CKV], qn_ref[...]).astype(BF16)
        qf = _dot(qn, wq_ref[...])
        ckvn = _rms(proj[:, C_CKV:C_KR], kvn_ref[...]).astype(BF16)
        kf = _dot(ckvn, wk_ref[...])
        kr = proj[:, C_KR:IN_COLS]
        rope = (lane >= ROPE_LANE0) & (lane < ROPE_LANE1)
        k_rope = jnp.where(rope, kr * cos_t + pltpu.roll(kr, LANES // 2, axis=1) * sin_t, 0.0)
        for h in range(MLA_HEADS):
            blk = slice(h * HEAD_BLOCK, (h + 1) * HEAD_BLOCK)
            qh = qf[:, blk]
            swapped = pltpu.roll(qh, LANES - HALF_ROPE, axis=1)
            q_ref[rows, blk] = ((qh * cos_t + swapped * sin_q) * scale).astype(BF16)
            k_ref[rows, blk] = (kf[:, blk] + k_rope).astype(BF16)

        vt = _dot_nt(wvt_ref[...], ckvn).astype(BF16)
        if len(vt_ref.shape) == 2:
            vt_ref[:, rows] = vt
        else:
            tk = vt_ref.shape[-1]
            for j in range(sub // tk):
                vt_ref[rows.start // tk + j] = vt[:, j * tk:(j + 1) * tk]

    xbuf[0:CONV_HALO, :] = xbuf[tm:tm + CONV_HALO, :]


def _in_proj(x2, cos_t, sin_t, params, layer, *, batch, seq, tm, tk, chunk):
    t, d = x2.shape
    nt = t // tm
    tiles_per_batch = seq // tm
    if tk >= tm:
        per_kv = tk // tm
        vt_spec = pl.BlockSpec((None, None, MLA_WIDTH, tm),
                               lambda i: (i // tiles_per_batch, (i % tiles_per_batch) // per_kv, 0, i % per_kv))
    else:
        vt_spec = pl.BlockSpec((None, tm // tk, MLA_WIDTH, tk),
                               lambda i: (i // tiles_per_batch, i % tiles_per_batch, 0, 0))
    row = lambda i: (i, 0)
    col = lambda i: (0, i)
    of_layer = lambda arr: pl.BlockSpec((None,) + arr.shape[1:], lambda i: (layer, 0, 0))
    out_shape = (
        jax.ShapeDtypeStruct((t, ML_QK_WIDTH), BF16),
        jax.ShapeDtypeStruct((t, ML_QK_WIDTH), BF16),
        jax.ShapeDtypeStruct((t, ML_WIDTH), BF16),
        jax.ShapeDtypeStruct((t, ML_WIDTH), BF16),
        jax.ShapeDtypeStruct((3 * SUBLANES, t), F32),
        jax.ShapeDtypeStruct((t, MLA_HEADS * HEAD_BLOCK), BF16),
        jax.ShapeDtypeStruct((t, MLA_HEADS * HEAD_BLOCK), BF16),
        jax.ShapeDtypeStruct((batch, seq // tk, MLA_WIDTH, tk), BF16),
    )
    out_specs = (
        pl.BlockSpec((tm, ML_QK_WIDTH), row),
        pl.BlockSpec((tm, ML_QK_WIDTH), row),
        pl.BlockSpec((tm, ML_WIDTH), row),
        pl.BlockSpec((tm, ML_WIDTH), row),
        pl.BlockSpec((3 * SUBLANES, tm), col),
        pl.BlockSpec((tm, MLA_HEADS * HEAD_BLOCK), row),
        pl.BlockSpec((tm, MLA_HEADS * HEAD_BLOCK), row),
        vt_spec,
    )
    in_specs = [pl.BlockSpec((tm, d), row), pl.BlockSpec((tm, LANES), row), pl.BlockSpec((tm, LANES), row)]
    in_specs += [of_layer(p) for p in params]
    return pl.pallas_call(
        functools.partial(_in_proj_kernel, tiles_per_batch=tiles_per_batch, chunk=chunk, sub=min(tm, 256)),
        out_shape=out_shape, grid=(nt,), in_specs=in_specs, out_specs=out_specs,
        scratch_shapes=[pltpu.VMEM((tm + CONV_HALO, QK_COLS), F32)],
        compiler_params=_params("arbitrary"), name="in_proj",
    )(x2, cos_t, sin_t, *params)


def _mlstm_kernel(q_ref, k_ref, v_ref, gs_ref, out_ref, c_ref, m_ref, *, chunk):
    L = chunk

    @pl.when(pl.program_id(1) == 0)
    def _():
        c_ref[...] = jnp.zeros_like(c_ref)
        m_ref[...] = jnp.zeros_like(m_ref)

    b, g, big_g = gs_ref[0:SUBLANES, :], gs_ref[SUBLANES:2 * SUBLANES, :], gs_ref[2 * SUBLANES:, :]
    m_prev = jnp.concatenate([m_ref[...]] * (L // LANES), axis=1)
    big_m = jnp.maximum(big_g, m_prev)
    inter_w = jnp.exp(m_prev - big_m)
    floor = jnp.exp(-(b + big_m))
    m_end = big_m[:, L - 1:L]
    w_state = jnp.exp(g - m_end)
    m_ref[...] = jnp.broadcast_to(b[:, L - 1:L] + m_end, m_ref.shape)
    stats = [big_m, inter_w, floor, w_state]
    pad = jnp.zeros((LANES - len(stats) * SUBLANES, L), F32)
    cols = jnp.concatenate(stats + [pad], axis=0).T

    row = lax.broadcasted_iota(jnp.int32, (L, L), 0)
    col = lax.broadcasted_iota(jnp.int32, (L, L), 1)
    causal = row >= col
    lane = lax.broadcasted_iota(jnp.int32, (L, LANES), 1)
    ones_blk = jnp.where(lane == 0, 1.0, 0.0).astype(BF16)

    for h in range(ML_HEADS):
        pair, lo = h // 2, (h % 2) * ML_QK
        blk = slice(pair * LANES, (pair + 1) * LANES)
        mine = (lane >= lo) & (lane < lo + ML_QK)
        qm = jnp.where(mine, q_ref[:, blk], jnp.zeros((), BF16))
        k_blk = k_ref[:, blk]
        s = _dot_nt(qm, k_blk)
        dw = jnp.where(causal, jnp.exp(g[h:h + 1, :] - cols[:, h:h + 1]), 0.0)
        v_ext = jnp.concatenate([v_ref[:, h * ML_V:(h + 1) * ML_V], ones_blk], axis=1)
        c_pair = c_ref[pair]
        iw = cols[:, SUBLANES + h:SUBLANES + h + 1]
        num = _dot((s * dw).astype(BF16), v_ext) + iw * _dot(qm, c_pair.astype(BF16))
        den = num[:, ML_V:ML_V + 1]
        hh = num[:, :ML_V] / jnp.maximum(jnp.abs(den), cols[:, 2 * SUBLANES + h:2 * SUBLANES + h + 1])
        out_ref[:, h * ML_V:(h + 1) * ML_V] = hh.astype(out_ref.dtype)

        kw = (k_blk.astype(F32) * cols[:, 3 * SUBLANES + h:3 * SUBLANES + h + 1]).astype(BF16)
        upd = _dot_tn(kw, v_ext)
        rows = slice(lo, lo + ML_QK)
        c_ref[pair, rows, :] = inter_w[h:h + 1, L - 1:L] * c_pair[rows, :] + upd[rows, :]


def _mlstm(q, k, v, gs, *, batch, seq, chunk):
    t = q.shape[0]
    nc = seq // chunk
    row = lambda b, c: (b * nc + c, 0)
    return pl.pallas_call(
        functools.partial(_mlstm_kernel, chunk=chunk),
        out_shape=jax.ShapeDtypeStruct((t, ML_WIDTH), BF16),
        grid=(batch, nc),
        in_specs=[pl.BlockSpec((chunk, ML_QK_WIDTH), row), pl.BlockSpec((chunk, ML_QK_WIDTH), row),
                  pl.BlockSpec((chunk, ML_WIDTH), row),
                  pl.BlockSpec((3 * SUBLANES, chunk), lambda b, c: (0, b * nc + c))],
        out_specs=pl.BlockSpec((chunk, ML_WIDTH), row),
        scratch_shapes=[pltpu.VMEM((ML_HEADS // 2, LANES, 2 * LANES), F32),
                        pltpu.VMEM((SUBLANES, LANES), F32)],
        compiler_params=_params("parallel", "arbitrary"), name="mlstm",
    )(q, k, v, gs)


def _attn_kernel(q_ref, k_ref, vt_ref, out_ref, s_sc, smax_sc, m_sc, acc_sc, *, tb, hp):
    last = pl.program_id(2)
    heads = range(hp)
    ones_rows = jnp.where(lax.broadcasted_iota(jnp.int32, (16, tb), 0) == 0, 1.0, 0.0).astype(BF16)

    def produce(j, slot, h, diagonal):
        kb = k_ref[pl.ds(pl.multiple_of(j * tb, tb), tb), h * HEAD_BLOCK:(h + 1) * HEAD_BLOCK]
        s = _dot_nt(kb, q_ref[:, h * HEAD_BLOCK:(h + 1) * HEAD_BLOCK])
        if diagonal:
            key = lax.broadcasted_iota(jnp.int32, s.shape, 0)
            qry = lax.broadcasted_iota(jnp.int32, s.shape, 1)
            s = jnp.where(key <= qry, s, -jnp.inf)
        s_sc[slot, h] = s
        smax_sc[slot, h] = jnp.max(s, axis=0, keepdims=True)

    def consume(j, slot, h):
        m = m_sc[h]
        m_new = jnp.maximum(m, smax_sc[slot, h])
        p = jnp.exp2(s_sc[slot, h] - m_new).astype(BF16)
        v_ext = jnp.concatenate([vt_ref[j, h * MLA_V:(h + 1) * MLA_V, :], ones_rows], axis=0)
        acc_sc[h] = jnp.exp2(m - m_new) * acc_sc[h] + _dot(v_ext, p)
        m_sc[h] = m_new

    def stage(j, slot, next_is_diagonal):
        for h in heads:
            produce(j + 1, 1 - slot, h, next_is_diagonal)
            consume(j, slot, h)

    def finish(slot):
        for h in heads:
            consume(last, slot, h)
        o_t = jnp.concatenate([acc_sc[h, :MLA_V, :] / acc_sc[h, MLA_V:MLA_V + 1, :] for h in heads], axis=0)
        out_ref[...] = o_t.T.astype(out_ref.dtype)

    m_sc[...] = jnp.full(m_sc.shape, -jnp.inf, F32)
    acc_sc[...] = jnp.zeros(acc_sc.shape, F32)

    @pl.when(last == 0)
    def _():
        for h in heads:
            produce(0, 0, h, True)
        finish(0)

    @pl.when(last > 0)
    def _():
        for h in heads:
            produce(0, 0, h, False)

        def four_stages(t, carry):
            for u in range(4):
                stage(4 * t + u, u % 2, False)
            return carry

        def two_stages(t, carry):
            stage(2 * t, 0, False)
            stage(2 * t + 1, 1, False)
            return carry

        quads = (last - 1) // 4
        lax.fori_loop(0, quads, four_stages, 0)
        lax.fori_loop(2 * quads, (last - 1) // 2, two_stages, 0)

        @pl.when(last % 2 == 1)
        def _():
            stage(last - 1, 0, True)
            finish(1)

        @pl.when(last % 2 == 0)
        def _():
            stage(last - 2, 0, False)
            stage(last - 1, 1, True)
            finish(0)


def _attention(q, k, vt, *, batch, seq, tb, hp):
    t = q.shape[0]
    nb = seq // tb
    return pl.pallas_call(
        functools.partial(_attn_kernel, tb=tb, hp=hp),
        out_shape=jax.ShapeDtypeStruct((t, MLA_WIDTH), BF16),
        grid=(batch, MLA_HEADS // hp, nb),
        in_specs=[pl.BlockSpec((tb, hp * HEAD_BLOCK), lambda b, p, i: (b * nb + i, p)),
                  pl.BlockSpec((seq, hp * HEAD_BLOCK), lambda b, p, i: (b, p)),
                  pl.BlockSpec((None, nb, hp * MLA_V, tb), lambda b, p, i: (b, 0, p, 0))],
        out_specs=pl.BlockSpec((tb, hp * MLA_V), lambda b, p, i: (b * nb + i, p)),
        scratch_shapes=[pltpu.VMEM((2, hp, tb, tb), F32), pltpu.VMEM((2, hp, 1, tb), F32),
                        pltpu.VMEM((hp, 1, tb), F32), pltpu.VMEM((hp, MLA_V + 16, tb), F32)],
        compiler_params=_params("parallel", "parallel", "arbitrary"), name="attn",
    )(q, k, vt)


def _mix_ffn_kernel(hml_ref, o_ref, hmla_ref, x_ref, hn_ref, wa_ref, wb_ref, gpost_ref, gpre_ref,
                    wu_ref, wd_ref, gmlp_ref, out_ref, m_sc, acc_sc, *, sub):
    j = pl.program_id(1)

    @pl.when(j == 0)
    def _():
        blocks = [slice(r, r + sub) for r in range(0, out_ref.shape[0], sub)]
        mixes = []
        for rows in blocks:
            heads = []
            for h in range(ML_HEADS):
                hs = slice(h * ML_V, (h + 1) * ML_V)
                hh = _rms(hml_ref[rows, hs].astype(F32), hn_ref[:, hs]) * jax.nn.sigmoid(o_ref[rows, hs].astype(F32))
                heads.append(hh.astype(BF16))
            mixes.append(_dot(jnp.concatenate(heads, axis=1), wa_ref[...]) + _dot(hmla_ref[rows, :], wb_ref[...]))
        for rows, mix in zip(blocks, mixes):
            x1 = x_ref[rows, :] + _rms(mix, gpost_ref[...])
            out_ref[rows, :] = x1
            m_sc[rows, :] = _rms(x1, gpre_ref[...]).astype(m_sc.dtype)
        acc_sc[...] = jnp.zeros(acc_sc.shape, F32)

    h = jnp.square(jnp.maximum(_dot(m_sc[...], wu_ref[...]), 0.0)).astype(BF16)
    acc_sc[...] += _dot(h, wd_ref[...])

    @pl.when(j == pl.num_programs(1) - 1)
    def _():
        out_ref[...] += _rms(acc_sc[...], gmlp_ref[...])


def _mix_ffn(hml, o, hmla, x2, params, layer, *, tm, tf):
    hn, wo, gpost, gpre, wu, wd, gmlp = params
    t, d = x2.shape
    dff = wu.shape[2]
    row = lambda i, j: (i, 0)
    vec = lambda arr: pl.BlockSpec((None,) + arr.shape[1:], lambda i, j: (layer, 0, 0))
    return pl.pallas_call(
        functools.partial(_mix_ffn_kernel, sub=min(tm, 128)),
        out_shape=jax.ShapeDtypeStruct((t, d), F32),
        grid=(t // tm, dff // tf),
        in_specs=[pl.BlockSpec((tm, ML_WIDTH), row), pl.BlockSpec((tm, ML_WIDTH), row),
                  pl.BlockSpec((tm, MLA_WIDTH), row), pl.BlockSpec((tm, d), row),
                  vec(hn),
                  pl.BlockSpec((None, ML_WIDTH, d), lambda i, j: (layer, 0, 0)),
                  pl.BlockSpec((None, MLA_WIDTH, d), lambda i, j: (layer, 1, 0)),
                  vec(gpost), vec(gpre),
                  pl.BlockSpec((None, d, tf), lambda i, j: (layer, 0, j)),
                  pl.BlockSpec((None, tf, d), lambda i, j: (layer, j, 0)), vec(gmlp)],
        out_specs=pl.BlockSpec((tm, d), row),
        scratch_shapes=[pltpu.VMEM((tm, d), BF16), pltpu.VMEM((tm, d), F32)],
        compiler_params=_params("parallel", "arbitrary"), name="mix_ffn",
    )(hml, o, hmla, x2, hn, wo, wo, gpost, gpre, wu, wd, gmlp)


def _layout_in_proj(w_in, b_gates):
    depth, d, _ = w_in.shape
    o0 = QK_COLS
    o1 = o0 + ML_WIDTH
    o2 = o1 + ML_WIDTH
    o3 = o2 + 2 * ML_HEADS
    o4 = o3 + MLA_Q_RANK
    o5 = o4 + MLA_KV_RANK
    gates, k_r = w_in[..., o2:o3], w_in[..., o5:]
    pad = jnp.zeros((depth, d, LANES - ROPE_LANE1), w_in.dtype)
    kr_blk = jnp.concatenate([-k_r[..., HALF_ROPE:], k_r[..., :HALF_ROPE], pad, k_r, pad], axis=2)
    win = jnp.concatenate([w_in[..., :o2], w_in[..., o3:o5], kr_blk], axis=2).astype(BF16)
    gates_t = jnp.swapaxes(gates, 1, 2)
    zrows = jnp.zeros((depth, SUBLANES - ML_HEADS, d), w_in.dtype)
    wg = jnp.concatenate([gates_t[:, :ML_HEADS], zrows, gates_t[:, ML_HEADS:], zrows], axis=1).astype(BF16)
    zb = jnp.zeros((depth, SUBLANES - ML_HEADS), b_gates.dtype)
    gbias = jnp.concatenate([b_gates[:, :ML_HEADS], zb, b_gates[:, ML_HEADS:], zb], axis=1)[..., None]
    return win, wg, gbias


def _layout_mla(w_uq, w_ukv):
    depth, rq, _ = w_uq.shape
    rkv = w_ukv.shape[1]
    wq4 = w_uq.reshape(depth, rq, MLA_HEADS, MLA_NOPE + MLA_ROPE)
    nope, rope = wq4[..., :MLA_NOPE], wq4[..., MLA_NOPE:]
    pad = jnp.zeros((depth, rq, MLA_HEADS, HEAD_BLOCK - ROPE_LANE1 - HALF_ROPE), w_uq.dtype)
    wq = jnp.concatenate([nope, rope, rope[..., :HALF_ROPE], pad], axis=3)
    wq = wq.reshape(depth, rq, MLA_HEADS * HEAD_BLOCK).astype(BF16)
    wkv4 = w_ukv.reshape(depth, rkv, MLA_HEADS, MLA_NOPE + MLA_V)
    kpad = jnp.zeros((depth, rkv, MLA_HEADS, HEAD_BLOCK - MLA_NOPE), w_ukv.dtype)
    wk = jnp.concatenate([wkv4[..., :MLA_NOPE], kpad], axis=3).reshape(depth, rkv, MLA_HEADS * HEAD_BLOCK).astype(BF16)
    wvt = jnp.swapaxes(wkv4[..., MLA_NOPE:].reshape(depth, rkv, MLA_WIDTH), 1, 2).astype(BF16)
    return wq, wk, wvt


def kernel(x, positions, norm_pre_mix, w_in, b_gates, conv_w, conv_b, ml_head_norm, q_norm, w_uq, kv_norm, w_ukv,
           w_out, norm_post_mix, norm_pre_mlp, w_up, w_down, norm_post_mlp):
    batch, seq, d = x.shape
    depth = w_in.shape[0]
    t = batch * seq
    tm_in = min(1024, seq)
    tb = min(256, seq)
    hp = 8
    chunk = min(512, seq)
    tm_ffn = min(1024, t)
    tf = 1024

    rows = lambda v: v[:, None, :]
    win, wg, gbias = _layout_in_proj(w_in, b_gates)
    wq, wk, wvt = _layout_mla(w_uq, w_ukv)
    in_params = (rows(norm_pre_mix), win, wg, gbias, conv_w, rows(conv_b), rows(q_norm), wq, rows(kv_norm), wk, wvt)
    mix_params = (rows(ml_head_norm), w_out.astype(BF16), rows(norm_post_mix), rows(norm_pre_mlp),
                  w_up.astype(BF16), w_down.astype(BF16), rows(norm_post_mlp))

    cos_t, sin_t = _rope_tables(positions)
    x2 = x.reshape(t, d)
    for l in range(depth):
        qml, kml, v, o, gs, q, k, vt = _in_proj(x2, cos_t, sin_t, in_params, l,
                                                batch=batch, seq=seq, tm=tm_in, tk=tb, chunk=chunk)
        hml = _mlstm(qml, kml, v, gs, batch=batch, seq=seq, chunk=chunk)
        hmla = _attention(q, k, vt, batch=batch, seq=seq, tb=tb, hp=hp)
        x2 = _mix_ffn(hml, o, hmla, x2, mix_params, l, tm=tm_ffn, tf=tf)
    return x2.reshape(batch, seq, d)
```

```python
import functools

import jax
import jax.numpy as jnp
from jax import lax
from jax.experimental import pallas as pl
from jax.experimental.pallas import tpu as pltpu

F32 = jnp.float32
BF16 = jnp.bfloat16

EPS = 1e-6
ML_HEADS, ML_QK, ML_V, ML_CONV = 4, 64, 128, 4
MLA_HEADS, MLA_NOPE, MLA_ROPE, MLA_V = 8, 64, 32, 64
MLA_Q_RANK, MLA_KV_RANK = 256, 128
ROPE_THETA = 10000.0
LOG2_E = 1.4426950408889634

ML_WIDTH = ML_HEADS * ML_V
ML_QK_WIDTH = ML_HEADS * ML_QK
QK_COLS = 2 * ML_QK_WIDTH
MLA_WIDTH = MLA_HEADS * MLA_V

LANES = 128
SUBLANES = 8
HEAD_BLOCK = LANES
ROPE_LANE0 = MLA_NOPE
ROPE_LANE1 = MLA_NOPE + MLA_ROPE
HALF_ROPE = MLA_ROPE // 2
C_QK, C_V, C_O = 0, QK_COLS, QK_COLS + ML_WIDTH
C_CQ = C_O + ML_WIDTH
C_CKV = C_CQ + MLA_Q_RANK
C_KR = C_CKV + MLA_KV_RANK
IN_COLS = C_KR + LANES
CONV_HALO = SUBLANES

VMEM_LIMIT = 56 * 1024 * 1024


def _params(*sem):
    return pltpu.CompilerParams(dimension_semantics=sem, vmem_limit_bytes=VMEM_LIMIT)


def _rms(x, g):
    return x * lax.rsqrt(jnp.mean(x * x, axis=-1, keepdims=True) + EPS) * g


def _log_sigmoid(x):
    return -(jnp.maximum(-x, 0.0) + jnp.log1p(jnp.exp(-jnp.abs(x))))


def _dot(a, b):
    return jnp.dot(a, b, preferred_element_type=F32)


def _dot_nt(a, b):
    return lax.dot_general(a, b, (((1,), (1,)), ((), ())), preferred_element_type=F32)


def _dot_tn(a, b):
    return lax.dot_general(a, b, (((0,), (0,)), ((), ())), preferred_element_type=F32)


def _rope_table_kernel(pos_ref, inv_ref, cos_ref, sin_ref):
    pos = pos_ref[...].astype(F32)
    slot = lax.broadcasted_iota(jnp.int32, cos_ref.shape, 1) // HALF_ROPE
    ang = jnp.zeros(cos_ref.shape, F32)
    for t in range(pos.shape[1]):
        ang = jnp.where(slot == t, pos[:, t:t + 1], ang)
    ang = ang * inv_ref[...]
    cos_ref[...] = jnp.cos(ang)
    sin_ref[...] = jnp.sin(ang)


def _rope_tables(positions):
    t = positions.size
    per_row = LANES // HALF_ROPE
    rows = t // per_row
    tr = min(rows, 512)
    inv = 1.0 / (ROPE_THETA ** (jnp.arange(0, MLA_ROPE, 2, dtype=F32) / MLA_ROPE))
    cos_c, sin_c = pl.pallas_call(
        _rope_table_kernel,
        out_shape=(jax.ShapeDtypeStruct((rows, LANES), F32),) * 2,
        grid=(rows // tr,),
        in_specs=[pl.BlockSpec((tr, per_row), lambda i: (i, 0)),
                  pl.BlockSpec((1, LANES), lambda i: (0, 0))],
        out_specs=(pl.BlockSpec((tr, LANES), lambda i: (i, 0)),) * 2,
        compiler_params=_params("parallel"),
        name="rope_tables",
    )(positions.reshape(rows, per_row), jnp.tile(inv, per_row).reshape(1, LANES))
    cos_h, sin_h = cos_c.reshape(t, HALF_ROPE), sin_c.reshape(t, HALF_ROPE)
    ones = jnp.ones((t, ROPE_LANE0), F32)
    zeros = jnp.zeros((t, ROPE_LANE0), F32)
    pad = jnp.zeros((t, LANES - ROPE_LANE1), F32)
    return (jnp.concatenate([ones, cos_h, cos_h, pad], axis=1), jnp.concatenate([zeros, sin_h, sin_h, pad], axis=1))


def _scan_lanes(x, op, identity, segment):
    pos = lax.broadcasted_iota(jnp.int32, x.shape, 1) % segment
    step = 1
    while step < segment:
        x = op(x, jnp.where(pos >= step, pltpu.roll(x, step, axis=1), identity))
        step *= 2
    return x


def _in_proj_kernel(x_ref, cos_ref, sin_ref, g_ref, win_ref, wg_ref, gbias_ref, cw_ref, cb_ref, qn_ref, wq_ref,
                    kvn_ref, wk_ref, wvt_ref,
                    qml_ref, kml_ref, v_ref, o_ref, gs_ref, q_ref, k_ref, vt_ref, xbuf,
                    *, tiles_per_batch, chunk, sub):
    tm = x_ref.shape[0]
    scale = (MLA_NOPE + MLA_ROPE) ** -0.5 * LOG2_E

    @pl.when(pl.program_id(0) % tiles_per_batch == 0)
    def _():
        xbuf[0:CONV_HALO, :] = jnp.zeros((CONV_HALO, QK_COLS), F32)

    blocks = [slice(r, r + sub) for r in range(0, tm, sub)]
    acts = [_rms(x_ref[rows, :], g_ref[...]).astype(BF16) for rows in blocks]
    projs = [_dot(a, win_ref[...]) for a in acts]

    gates = jnp.concatenate([_dot_nt(wg_ref[...], a) for a in acts], axis=1) + gbias_ref[...]
    b = _scan_lanes(_log_sigmoid(gates[SUBLANES:]), jnp.add, 0.0, chunk)
    g = gates[:SUBLANES] - b
    gs_ref[...] = jnp.concatenate([b, g, _scan_lanes(g, jnp.maximum, -jnp.inf, chunk)], axis=0)

    for rows, a, proj in zip(blocks, acts, projs):
        v_ref[rows, :] = proj[:, C_V:C_O].astype(BF16)
        o_ref[rows, :] = proj[:, C_O:C_CQ].astype(BF16)

        top = CONV_HALO + rows.start
        xbuf[top:top + sub, :] = proj[:, C_QK:C_V]
        cw = cw_ref[...]
        y = cb_ref[...]
        for tap in range(ML_CONV):
            start = top - (ML_CONV - 1 - tap)
            y = y + xbuf[start:start + sub, :] * cw[tap:tap + 1, :]
        qk = y * jax.nn.sigmoid(y)
        qml_ref[rows, :] = qk[:, :ML_QK_WIDTH].astype(BF16)
        kml_ref[rows, :] = (qk[:, ML_QK_WIDTH:] * (ML_QK ** -0.5)).astype(BF16)

        cos_t = cos_ref[rows, :]
        lane = lax.broadcasted_iota(jnp.int32, cos_t.shape, 1)
        first_half = lane < ROPE_LANE0 + HALF_ROPE
        sin_t = sin_ref[rows, :]
        sin_q = jnp.where(first_half, -sin_t, sin_t)
        qn = _rms(proj[:, C_CQ:C_CKV], qn_ref[...]).astype(BF16)
        qf = _dot(qn, wq_ref[...])
        ckvn = _rms(proj[:, C_CKV:C_KR], kvn_ref[...]).astype(BF16)
        kf = _dot(ckvn, wk_ref[...])
        kr = proj[:, C_KR:IN_COLS]
        rope = (lane >= ROPE_LANE0) & (lane < ROPE_LANE1)
        k_rope = jnp.where(rope, kr * cos_t + pltpu.roll(kr, LANES // 2, axis=1) * sin_t, 0.0)
        for h in range(MLA_HEADS):
            blk = slice(h * HEAD_BLOCK, (h + 1) * HEAD_BLOCK)
            qh = qf[:, blk]
            swapped = pltpu.roll(qh, LANES - HALF_ROPE, axis=1)
            q_ref[rows, blk] = ((qh * cos_t + swapped * sin_q) * scale).astype(BF16)
            k_ref[rows, blk] = (kf[:, blk] + k_rope).astype(BF16)

        vt = _dot_nt(wvt_ref[...], ckvn).astype(BF16)
        if len(vt_ref.shape) == 2:
            vt_ref[:, rows] = vt
        else:
            tk = vt_ref.shape[-1]
            for j in range(sub // tk):
                vt_ref[rows.start // tk + j] = vt[:, j * tk:(j + 1) * tk]

    xbuf[0:CONV_HALO, :] = xbuf[tm:tm + CONV_HALO, :]


def _in_proj(x2, cos_t, sin_t, params, layer, *, batch, seq, tm, tk, chunk):
    t, d = x2.shape
    nt = t // tm
    tiles_per_batch = seq // tm
    if tk >= tm:
        per_kv = tk // tm
        vt_spec = pl.BlockSpec((None, None, MLA_WIDTH, tm),
                               lambda i: (i // tiles_per_batch, (i % tiles_per_batch) // per_kv, 0, i % per_kv))
    else:
        vt_spec = pl.BlockSpec((None, tm // tk, MLA_WIDTH, tk),
                               lambda i: (i // tiles_per_batch, i % tiles_per_batch, 0, 0))
    row = lambda i: (i, 0)
    col = lambda i: (0, i)
    of_layer = lambda arr: pl.BlockSpec((None,) + arr.shape[1:], lambda i: (layer, 0, 0))
    out_shape = (
        jax.ShapeDtypeStruct((t, ML_QK_WIDTH), BF16),
        jax.ShapeDtypeStruct((t, ML_QK_WIDTH), BF16),
        jax.ShapeDtypeStruct((t, ML_WIDTH), BF16),
        jax.ShapeDtypeStruct((t, ML_WIDTH), BF16),
        jax.ShapeDtypeStruct((3 * SUBLANES, t), F32),
        jax.ShapeDtypeStruct((t, MLA_HEADS * HEAD_BLOCK), BF16),
        jax.ShapeDtypeStruct((t, MLA_HEADS * HEAD_BLOCK), BF16),
        jax.ShapeDtypeStruct((batch, seq // tk, MLA_WIDTH, tk), BF16),
    )
    out_specs = (
        pl.BlockSpec((tm, ML_QK_WIDTH), row),
        pl.BlockSpec((tm, ML_QK_WIDTH), row),
        pl.BlockSpec((tm, ML_WIDTH), row),
        pl.BlockSpec((tm, ML_WIDTH), row),
        pl.BlockSpec((3 * SUBLANES, tm), col),
        pl.BlockSpec((tm, MLA_HEADS * HEAD_BLOCK), row),
        pl.BlockSpec((tm, MLA_HEADS * HEAD_BLOCK), row),
        vt_spec,
    )
    in_specs = [pl.BlockSpec((tm, d), row), pl.BlockSpec((tm, LANES), row), pl.BlockSpec((tm, LANES), row)]
    in_specs += [of_layer(p) for p in params]
    return pl.pallas_call(
        functools.partial(_in_proj_kernel, tiles_per_batch=tiles_per_batch, chunk=chunk, sub=min(tm, 256)),
        out_shape=out_shape, grid=(nt,), in_specs=in_specs, out_specs=out_specs,
        scratch_shapes=[pltpu.VMEM((tm + CONV_HALO, QK_COLS), F32)],
        compiler_params=_params("arbitrary"), name="in_proj",
    )(x2, cos_t, sin_t, *params)


def _mlstm_kernel(q_ref, k_ref, v_ref, gs_ref, out_ref, c_ref, m_ref, *, chunk):
    L = chunk

    @pl.when(pl.program_id(1) == 0)
    def _():
        c_ref[...] = jnp.zeros_like(c_ref)
        m_ref[...] = jnp.zeros_like(m_ref)

    b, g, big_g = gs_ref[0:SUBLANES, :], gs_ref[SUBLANES:2 * SUBLANES, :], gs_ref[2 * SUBLANES:, :]
    m_prev = jnp.concatenate([m_ref[...]] * (L // LANES), axis=1)
    big_m = jnp.maximum(big_g, m_prev)
    inter_w = jnp.exp(m_prev - big_m)
    floor = jnp.exp(-(b + big_m))
    m_end = big_m[:, L - 1:L]
    w_state = jnp.exp(g - m_end)
    m_ref[...] = jnp.broadcast_to(b[:, L - 1:L] + m_end, m_ref.shape)
    stats = [big_m, inter_w, floor, w_state]
    pad = jnp.zeros((LANES - len(stats) * SUBLANES, L), F32)
    cols = jnp.concatenate(stats + [pad], axis=0).T

    row = lax.broadcasted_iota(jnp.int32, (L, L), 0)
    col = lax.broadcasted_iota(jnp.int32, (L, L), 1)
    causal = row >= col
    lane = lax.broadcasted_iota(jnp.int32, (L, LANES), 1)
    ones_blk = jnp.where(lane == 0, 1.0, 0.0).astype(BF16)

    for h in range(ML_HEADS):
        pair, lo = h // 2, (h % 2) * ML_QK
        blk = slice(pair * LANES, (pair + 1) * LANES)
        mine = (lane >= lo) & (lane < lo + ML_QK)
        qm = jnp.where(mine, q_ref[:, blk], jnp.zeros((), BF16))
        k_blk = k_ref[:, blk]
        s = _dot_nt(qm, k_blk)
        dw = jnp.where(causal, jnp.exp(g[h:h + 1, :] - cols[:, h:h + 1]), 0.0)
        v_ext = jnp.concatenate([v_ref[:, h * ML_V:(h + 1) * ML_V], ones_blk], axis=1)
        c_pair = c_ref[pair]
        iw = cols[:, SUBLANES + h:SUBLANES + h + 1]
        num = _dot((s * dw).astype(BF16), v_ext) + iw * _dot(qm, c_pair.astype(BF16))
        den = num[:, ML_V:ML_V + 1]
        hh = num[:, :ML_V] / jnp.maximum(jnp.abs(den), cols[:, 2 * SUBLANES + h:2 * SUBLANES + h + 1])
        out_ref[:, h * ML_V:(h + 1) * ML_V] = hh.astype(out_ref.dtype)

        kw = (k_blk.astype(F32) * cols[:, 3 * SUBLANES + h:3 * SUBLANES + h + 1]).astype(BF16)
        upd = _dot_tn(kw, v_ext)
        rows = slice(lo, lo + ML_QK)
        c_ref[pair, rows, :] = inter_w[h:h + 1, L - 1:L] * c_pair[rows, :] + upd[rows, :]


def _mlstm(q, k, v, gs, *, batch, seq, chunk):
    t = q.shape[0]
    nc = seq // chunk
    row = lambda b, c: (b * nc + c, 0)
    return pl.pallas_call(
        functools.partial(_mlstm_kernel, chunk=chunk),
        out_shape=jax.ShapeDtypeStruct((t, ML_WIDTH), BF16),
        grid=(batch, nc),
        in_specs=[pl.BlockSpec((chunk, ML_QK_WIDTH), row), pl.BlockSpec((chunk, ML_QK_WIDTH), row),
                  pl.BlockSpec((chunk, ML_WIDTH), row),
                  pl.BlockSpec((3 * SUBLANES, chunk), lambda b, c: (0, b * nc + c))],
        out_specs=pl.BlockSpec((chunk, ML_WIDTH), row),
        scratch_shapes=[pltpu.VMEM((ML_HEADS // 2, LANES, 2 * LANES), F32),
                        pltpu.VMEM((SUBLANES, LANES), F32)],
        compiler_params=_params("parallel", "arbitrary"), name="mlstm",
    )(q, k, v, gs)


def _attn_kernel(q_ref, k_ref, vt_ref, out_ref, s_sc, smax_sc, m_sc, acc_sc, *, tb, hp):
    last = pl.program_id(2)
    heads = range(hp)
    ones_rows = jnp.where(lax.broadcasted_iota(jnp.int32, (16, tb), 0) == 0, 1.0, 0.0).astype(BF16)

    def produce(j, slot, h, diagonal):
        kb = k_ref[pl.ds(pl.multiple_of(j * tb, tb), tb), h * HEAD_BLOCK:(h + 1) * HEAD_BLOCK]
        s = _dot_nt(kb, q_ref[:, h * HEAD_BLOCK:(h + 1) * HEAD_BLOCK])
        if diagonal:
            key = lax.broadcasted_iota(jnp.int32, s.shape, 0)
            qry = lax.broadcasted_iota(jnp.int32, s.shape, 1)
            s = jnp.where(key <= qry, s, -jnp.inf)
        s_sc[slot, h] = s
        smax_sc[slot, h] = jnp.max(s, axis=0, keepdims=True)

    def consume(j, slot, h):
        m = m_sc[h]
        m_new = jnp.maximum(m, smax_sc[slot, h])
        p = jnp.exp2(s_sc[slot, h] - m_new).astype(BF16)
        v_ext = jnp.concatenate([vt_ref[j, h * MLA_V:(h + 1) * MLA_V, :], ones_rows], axis=0)
        acc_sc[h] = jnp.exp2(m - m_new) * acc_sc[h] + _dot(v_ext, p)
        m_sc[h] = m_new

    def stage(j, slot, next_is_diagonal):
        for h in heads:
            produce(j + 1, 1 - slot, h, next_is_diagonal)
            consume(j, slot, h)

    def finish(slot):
        for h in heads:
            consume(last, slot, h)
        o_t = jnp.concatenate([acc_sc[h, :MLA_V, :] / acc_sc[h, MLA_V:MLA_V + 1, :] for h in heads], axis=0)
        out_ref[...] = o_t.T.astype(out_ref.dtype)

    m_sc[...] = jnp.full(m_sc.shape, -jnp.inf, F32)
    acc_sc[...] = jnp.zeros(acc_sc.shape, F32)

    @pl.when(last == 0)
    def _():
        for h in heads:
            produce(0, 0, h, True)
        finish(0)

    @pl.when(last > 0)
    def _():
        for h in heads:
            produce(0, 0, h, False)

        def stages(group):
            def body(t, carry):
                for u in range(group):
                    stage(group * t + u, u % 2, False)
                return carry
            return body

        pairs = (last - 1) // 2
        lax.fori_loop(0, pairs // 4, stages(8), 0)
        lax.fori_loop(pairs // 4 * 2, pairs // 2, stages(4), 0)
        lax.fori_loop(pairs // 2 * 2, pairs, stages(2), 0)

        @pl.when(last % 2 == 1)
        def _():
            stage(last - 1, 0, True)
            finish(1)

        @pl.when(last % 2 == 0)
        def _():
            stage(last - 2, 0, False)
            stage(last - 1, 1, True)
            finish(0)


def _attention(q, k, vt, *, batch, seq, tb, hp):
    t = q.shape[0]
    nb = seq // tb
    return pl.pallas_call(
        functools.partial(_attn_kernel, tb=tb, hp=hp),
        out_shape=jax.ShapeDtypeStruct((t, MLA_WIDTH), BF16),
        grid=(batch, MLA_HEADS // hp, nb),
        in_specs=[pl.BlockSpec((tb, hp * HEAD_BLOCK), lambda b, p, i: (b * nb + i, p)),
                  pl.BlockSpec((seq, hp * HEAD_BLOCK), lambda b, p, i: (b, p)),
                  pl.BlockSpec((None, nb, hp * MLA_V, tb), lambda b, p, i: (b, 0, p, 0))],
        out_specs=pl.BlockSpec((tb, hp * MLA_V), lambda b, p, i: (b * nb + i, p)),
        scratch_shapes=[pltpu.VMEM((2, hp, tb, tb), F32), pltpu.VMEM((2, hp, 1, tb), F32),
                        pltpu.VMEM((hp, 1, tb), F32), pltpu.VMEM((hp, MLA_V + 16, tb), F32)],
        compiler_params=_params("parallel", "parallel", "arbitrary"), name="attn",
    )(q, k, vt)


def _mix_ffn_kernel(hml_ref, o_ref, hmla_ref, x_ref, hn_ref, wa_ref, wb_ref, gpost_ref, gpre_ref,
                    wu_ref, wd_ref, gmlp_ref, out_ref, m_sc, acc_sc, *, sub):
    j = pl.program_id(1)

    def mlp_slice(m):
        h = jnp.square(jnp.maximum(_dot(m, wu_ref[...]), 0.0)).astype(BF16)
        return _dot(h, wd_ref[...])

    @pl.when(j == 0)
    def _():
        blocks = [slice(r, r + sub) for r in range(0, out_ref.shape[0], sub)]
        mixes = []
        for rows in blocks:
            heads = []
            for h in range(ML_HEADS):
                hs = slice(h * ML_V, (h + 1) * ML_V)
                hh = _rms(hml_ref[rows, hs].astype(F32), hn_ref[:, hs]) * jax.nn.sigmoid(o_ref[rows, hs].astype(F32))
                heads.append(hh.astype(BF16))
            mixes.append(_dot(jnp.concatenate(heads, axis=1), wa_ref[...]) + _dot(hmla_ref[rows, :], wb_ref[...]))
        for rows, mix in zip(blocks, mixes):
            x1 = x_ref[rows, :] + _rms(mix, gpost_ref[...])
            out_ref[rows, :] = x1
            m = _rms(x1, gpre_ref[...]).astype(m_sc.dtype)
            m_sc[rows, :] = m
            acc_sc[rows, :] = mlp_slice(m)

    @pl.when(j > 0)
    def _():
        acc_sc[...] += mlp_slice(m_sc[...])

    @pl.when(j == pl.num_programs(1) - 1)
    def _():
        out_ref[...] += _rms(acc_sc[...], gmlp_ref[...])


def _mix_ffn(hml, o, hmla, x2, params, layer, *, tm, tf):
    hn, wo, gpost, gpre, wu, wd, gmlp = params
    t, d = x2.shape
    dff = wu.shape[2]
    row = lambda i, j: (i, 0)
    vec = lambda arr: pl.BlockSpec((None,) + arr.shape[1:], lambda i, j: (layer, 0, 0))
    return pl.pallas_call(
        functools.partial(_mix_ffn_kernel, sub=min(tm, 256)),
        out_shape=jax.ShapeDtypeStruct((t, d), F32),
        grid=(t // tm, dff // tf),
        in_specs=[pl.BlockSpec((tm, ML_WIDTH), row), pl.BlockSpec((tm, ML_WIDTH), row),
                  pl.BlockSpec((tm, MLA_WIDTH), row), pl.BlockSpec((tm, d), row),
                  vec(hn),
                  pl.BlockSpec((None, ML_WIDTH, d), lambda i, j: (layer, 0, 0)),
                  pl.BlockSpec((None, MLA_WIDTH, d), lambda i, j: (layer, 1, 0)),
                  vec(gpost), vec(gpre),
                  pl.BlockSpec((None, d, tf), lambda i, j: (layer, 0, j)),
                  pl.BlockSpec((None, tf, d), lambda i, j: (layer, j, 0)), vec(gmlp)],
        out_specs=pl.BlockSpec((tm, d), row),
        scratch_shapes=[pltpu.VMEM((tm, d), BF16), pltpu.VMEM((tm, d), F32)],
        compiler_params=_params("parallel", "arbitrary"), name="mix_ffn",
    )(hml, o, hmla, x2, hn, wo, wo, gpost, gpre, wu, wd, gmlp)


def _layout_in_proj(w_in, b_gates):
    depth, d, _ = w_in.shape
    o0 = QK_COLS
    o1 = o0 + ML_WIDTH
    o2 = o1 + ML_WIDTH
    o3 = o2 + 2 * ML_HEADS
    o4 = o3 + MLA_Q_RANK
    o5 = o4 + MLA_KV_RANK
    gates, k_r = w_in[..., o2:o3], w_in[..., o5:]
    pad = jnp.zeros((depth, d, LANES - ROPE_LANE1), w_in.dtype)
    kr_blk = jnp.concatenate([-k_r[..., HALF_ROPE:], k_r[..., :HALF_ROPE], pad, k_r, pad], axis=2)
    win = jnp.concatenate([w_in[..., :o2], w_in[..., o3:o5], kr_blk], axis=2).astype(BF16)
    gates_t = jnp.swapaxes(gates, 1, 2)
    zrows = jnp.zeros((depth, SUBLANES - ML_HEADS, d), w_in.dtype)
    wg = jnp.concatenate([gates_t[:, :ML_HEADS], zrows, gates_t[:, ML_HEADS:], zrows], axis=1).astype(BF16)
    zb = jnp.zeros((depth, SUBLANES - ML_HEADS), b_gates.dtype)
    gbias = jnp.concatenate([b_gates[:, :ML_HEADS], zb, b_gates[:, ML_HEADS:], zb], axis=1)[..., None]
    return win, wg, gbias


def _layout_mla(w_uq, w_ukv):
    depth, rq, _ = w_uq.shape
    rkv = w_ukv.shape[1]
    wq4 = w_uq.reshape(depth, rq, MLA_HEADS, MLA_NOPE + MLA_ROPE)
    nope, rope = wq4[..., :MLA_NOPE], wq4[..., MLA_NOPE:]
    pad = jnp.zeros((depth, rq, MLA_HEADS, HEAD_BLOCK - ROPE_LANE1 - HALF_ROPE), w_uq.dtype)
    wq = jnp.concatenate([nope, rope, rope[..., :HALF_ROPE], pad], axis=3)
    wq = wq.reshape(depth, rq, MLA_HEADS * HEAD_BLOCK).astype(BF16)
    wkv4 = w_ukv.reshape(depth, rkv, MLA_HEADS, MLA_NOPE + MLA_V)
    kpad = jnp.zeros((depth, rkv, MLA_HEADS, HEAD_BLOCK - MLA_NOPE), w_ukv.dtype)
    wk = jnp.concatenate([wkv4[..., :MLA_NOPE], kpad], axis=3).reshape(depth, rkv, MLA_HEADS * HEAD_BLOCK).astype(BF16)
    wvt = jnp.swapaxes(wkv4[..., MLA_NOPE:].reshape(depth, rkv, MLA_WIDTH), 1, 2).astype(BF16)
    return wq, wk, wvt


def kernel(x, positions, norm_pre_mix, w_in, b_gates, conv_w, conv_b, ml_head_norm, q_norm, w_uq, kv_norm, w_ukv,
           w_out, norm_post_mix, norm_pre_mlp, w_up, w_down, norm_post_mlp):
    batch, seq, d = x.shape
    depth = w_in.shape[0]
    t = batch * seq
    tm_in = min(1024, seq)
    tb = min(256, seq)
    hp = 8
    chunk = min(512, seq)
    tm_ffn = min(1024, t)
    tf = 1024

    rows = lambda v: v[:, None, :]
    win, wg, gbias = _layout_in_proj(w_in, b_gates)
    wq, wk, wvt = _layout_mla(w_uq, w_ukv)
    in_params = (rows(norm_pre_mix), win, wg, gbias, conv_w, rows(conv_b), rows(q_norm), wq, rows(kv_norm), wk, wvt)
    mix_params = (rows(ml_head_norm), w_out.astype(BF16), rows(norm_post_mix), rows(norm_pre_mlp),
                  w_up.astype(BF16), w_down.astype(BF16), rows(norm_post_mlp))

    cos_t, sin_t = _rope_tables(positions)
    x2 = x.reshape(t, d)
    for l in range(depth):
        qml, kml, v, o, gs, q, k, vt = _in_proj(x2, cos_t, sin_t, in_params, l,
                                                batch=batch, seq=seq, tm=tm_in, tk=tb, chunk=chunk)
        hml = _mlstm(qml, kml, v, gs, batch=batch, seq=seq, chunk=chunk)
        hmla = _attention(q, k, vt, batch=batch, seq=seq, tb=tb, hp=hp)
        x2 = _mix_ffn(hml, o, hmla, x2, mix_params, l, tm=tm_ffn, tf=tf)
    return x2.reshape(batch, seq, d)
```

```python
import functools

import jax
import jax.numpy as jnp
from jax import lax
from jax.experimental import pallas as pl
from jax.experimental.pallas import tpu as pltpu

F32 = jnp.float32
BF16 = jnp.bfloat16

EPS = 1e-6
ML_HEADS, ML_QK, ML_V, ML_CONV = 4, 64, 128, 4
MLA_HEADS, MLA_NOPE, MLA_ROPE, MLA_V = 8, 64, 32, 64
MLA_Q_RANK, MLA_KV_RANK = 256, 128
ROPE_THETA = 10000.0
LOG2_E = 1.4426950408889634

ML_WIDTH = ML_HEADS * ML_V
ML_QK_WIDTH = ML_HEADS * ML_QK
QK_COLS = 2 * ML_QK_WIDTH
MLA_WIDTH = MLA_HEADS * MLA_V

LANES = 128
SUBLANES = 8
HEAD_BLOCK = LANES
ROPE_LANE0 = MLA_NOPE
ROPE_LANE1 = MLA_NOPE + MLA_ROPE
HALF_ROPE = MLA_ROPE // 2
C_QK, C_V, C_O = 0, QK_COLS, QK_COLS + ML_WIDTH
ML_COLS = C_O + ML_WIDTH
C_CQ = 0
C_CKV = C_CQ + MLA_Q_RANK
C_KR = C_CKV + MLA_KV_RANK
MLA_COLS = C_KR + LANES
CONV_HALO = SUBLANES

VMEM_LIMIT = 56 * 1024 * 1024


def _params(*sem):
    return pltpu.CompilerParams(dimension_semantics=sem, vmem_limit_bytes=VMEM_LIMIT)


def _rms(x, g):
    return x * lax.rsqrt(jnp.mean(x * x, axis=-1, keepdims=True) + EPS) * g


def _log_sigmoid(x):
    return -(jnp.maximum(-x, 0.0) + jnp.log1p(jnp.exp(-jnp.abs(x))))


def _dot(a, b):
    return jnp.dot(a, b, preferred_element_type=F32)


def _dot_nt(a, b):
    return lax.dot_general(a, b, (((1,), (1,)), ((), ())), preferred_element_type=F32)


def _dot_tn(a, b):
    return lax.dot_general(a, b, (((0,), (0,)), ((), ())), preferred_element_type=F32)


def _rope_table_kernel(pos_ref, inv_ref, cos_ref, sin_ref):
    pos = pos_ref[...].astype(F32)
    slot = lax.broadcasted_iota(jnp.int32, cos_ref.shape, 1) // HALF_ROPE
    ang = jnp.zeros(cos_ref.shape, F32)
    for t in range(pos.shape[1]):
        ang = jnp.where(slot == t, pos[:, t:t + 1], ang)
    ang = ang * inv_ref[...]
    cos_ref[...] = jnp.cos(ang)
    sin_ref[...] = jnp.sin(ang)


def _rope_tables(positions):
    t = positions.size
    per_row = LANES // HALF_ROPE
    rows = t // per_row
    tr = min(rows, 512)
    inv = 1.0 / (ROPE_THETA ** (jnp.arange(0, MLA_ROPE, 2, dtype=F32) / MLA_ROPE))
    cos_c, sin_c = pl.pallas_call(
        _rope_table_kernel,
        out_shape=(jax.ShapeDtypeStruct((rows, LANES), F32),) * 2,
        grid=(rows // tr,),
        in_specs=[pl.BlockSpec((tr, per_row), lambda i: (i, 0)),
                  pl.BlockSpec((1, LANES), lambda i: (0, 0))],
        out_specs=(pl.BlockSpec((tr, LANES), lambda i: (i, 0)),) * 2,
        compiler_params=_params("parallel"),
        name="rope_tables",
    )(positions.reshape(rows, per_row), jnp.tile(inv, per_row).reshape(1, LANES))
    cos_h, sin_h = cos_c.reshape(t, HALF_ROPE), sin_c.reshape(t, HALF_ROPE)
    ones = jnp.ones((t, ROPE_LANE0), F32)
    zeros = jnp.zeros((t, ROPE_LANE0), F32)
    pad = jnp.zeros((t, LANES - ROPE_LANE1), F32)
    return (jnp.concatenate([ones, cos_h, cos_h, pad], axis=1), jnp.concatenate([zeros, sin_h, sin_h, pad], axis=1))


def _scan_lanes(x, op, identity, segment):
    pos = lax.broadcasted_iota(jnp.int32, x.shape, 1) % segment
    step = 1
    while step < segment:
        x = op(x, jnp.where(pos >= step, pltpu.roll(x, step, axis=1), identity))
        step *= 2
    return x


def _in_proj_kernel(x_ref, cos_ref, sin_ref, g_ref, win_ref, wmla_ref, wg_ref, gbias_ref, cw_ref, cb_ref, qn_ref, wq_ref,
                    kvn_ref, wk_ref, wvt_ref,
                    qml_ref, kml_ref, v_ref, o_ref, gs_ref, q_ref, k_ref, vt_ref, xbuf,
                    *, tiles_per_batch, chunk, sub):
    tm = x_ref.shape[0]
    scale = (MLA_NOPE + MLA_ROPE) ** -0.5 * LOG2_E

    @pl.when(pl.program_id(0) % tiles_per_batch == 0)
    def _():
        xbuf[0:CONV_HALO, :] = jnp.zeros((CONV_HALO, QK_COLS), F32)

    blocks = [slice(r, r + sub) for r in range(0, tm, sub)]
    acts = [_rms(x_ref[rows, :], g_ref[...]).astype(BF16) for rows in blocks]
    projs = [_dot(a, win_ref[:, :ML_COLS]) for a in acts]
    projs_mla = [_dot(a, wmla_ref[...]) for a in acts]

    gates = jnp.concatenate([_dot_nt(wg_ref[...], a) for a in acts], axis=1) + gbias_ref[...]
    b = _scan_lanes(_log_sigmoid(gates[SUBLANES:]), jnp.add, 0.0, chunk)
    g = gates[:SUBLANES] - b
    gs_ref[...] = jnp.concatenate([b, g, _scan_lanes(g, jnp.maximum, -jnp.inf, chunk)], axis=0)

    for rows, proj, pmla in zip(blocks, projs, projs_mla):
        v_ref[rows, :] = proj[:, C_V:C_O].astype(BF16)
        o_ref[rows, :] = proj[:, C_O:ML_COLS].astype(BF16)

        top = CONV_HALO + rows.start
        xbuf[top:top + sub, :] = proj[:, C_QK:C_V]
        cw = cw_ref[...]
        y = cb_ref[...]
        for tap in range(ML_CONV):
            start = top - (ML_CONV - 1 - tap)
            y = y + xbuf[start:start + sub, :] * cw[tap:tap + 1, :]
        qk = y * jax.nn.sigmoid(y)
        qml_ref[rows, :] = qk[:, :ML_QK_WIDTH].astype(BF16)
        kml_ref[rows, :] = (qk[:, ML_QK_WIDTH:] * (ML_QK ** -0.5)).astype(BF16)

        cos_t = cos_ref[rows, :]
        lane = lax.broadcasted_iota(jnp.int32, cos_t.shape, 1)
        first_half = lane < ROPE_LANE0 + HALF_ROPE
        sin_t = sin_ref[rows, :]
        sin_q = jnp.where(first_half, -sin_t, sin_t)
        qn = _rms(pmla[:, C_CQ:C_CKV], qn_ref[...]).astype(BF16)
        qf = _dot(qn, wq_ref[...])
        ckvn = _rms(pmla[:, C_CKV:C_KR], kvn_ref[...]).astype(BF16)
        kf = _dot(ckvn, wk_ref[...])
        kr = pmla[:, C_KR:MLA_COLS]
        rope = (lane >= ROPE_LANE0) & (lane < ROPE_LANE1)
        k_rope = jnp.where(rope, kr * cos_t + pltpu.roll(kr, LANES // 2, axis=1) * sin_t, 0.0)
        for h in range(MLA_HEADS):
            blk = slice(h * HEAD_BLOCK, (h + 1) * HEAD_BLOCK)
            qh = qf[:, blk]
            swapped = pltpu.roll(qh, LANES - HALF_ROPE, axis=1)
            q_ref[rows, blk] = ((qh * cos_t + swapped * sin_q) * scale).astype(BF16)
            k_ref[rows, blk] = (kf[:, blk] + k_rope).astype(BF16)

        vt = _dot_nt(wvt_ref[...], ckvn).astype(BF16)
        if len(vt_ref.shape) == 2:
            vt_ref[:, rows] = vt
        else:
            tk = vt_ref.shape[-1]
            for j in range(sub // tk):
                vt_ref[rows.start // tk + j] = vt[:, j * tk:(j + 1) * tk]

    xbuf[0:CONV_HALO, :] = xbuf[tm:tm + CONV_HALO, :]


def _in_proj(x2, cos_t, sin_t, params, layer, *, batch, seq, tm, tk, chunk):
    t, d = x2.shape
    nt = t // tm
    tiles_per_batch = seq // tm
    if tk >= tm:
        per_kv = tk // tm
        vt_spec = pl.BlockSpec((None, None, MLA_WIDTH, tm),
                               lambda i: (i // tiles_per_batch, (i % tiles_per_batch) // per_kv, 0, i % per_kv))
    else:
        vt_spec = pl.BlockSpec((None, tm // tk, MLA_WIDTH, tk),
                               lambda i: (i // tiles_per_batch, i % tiles_per_batch, 0, 0))
    row = lambda i: (i, 0)
    col = lambda i: (0, i)
    of_layer = lambda arr: pl.BlockSpec((None,) + arr.shape[1:], lambda i: (layer, 0, 0))
    out_shape = (
        jax.ShapeDtypeStruct((t, ML_QK_WIDTH), BF16),
        jax.ShapeDtypeStruct((t, ML_QK_WIDTH), BF16),
        jax.ShapeDtypeStruct((t, ML_WIDTH), BF16),
        jax.ShapeDtypeStruct((t, ML_WIDTH), BF16),
        jax.ShapeDtypeStruct((3 * SUBLANES, t), F32),
        jax.ShapeDtypeStruct((t, MLA_HEADS * HEAD_BLOCK), BF16),
        jax.ShapeDtypeStruct((t, MLA_HEADS * HEAD_BLOCK), BF16),
        jax.ShapeDtypeStruct((batch, seq // tk, MLA_WIDTH, tk), BF16),
    )
    out_specs = (
        pl.BlockSpec((tm, ML_QK_WIDTH), row),
        pl.BlockSpec((tm, ML_QK_WIDTH), row),
        pl.BlockSpec((tm, ML_WIDTH), row),
        pl.BlockSpec((tm, ML_WIDTH), row),
        pl.BlockSpec((3 * SUBLANES, tm), col),
        pl.BlockSpec((tm, MLA_HEADS * HEAD_BLOCK), row),
        pl.BlockSpec((tm, MLA_HEADS * HEAD_BLOCK), row),
        vt_spec,
    )
    in_specs = [pl.BlockSpec((tm, d), row), pl.BlockSpec((tm, LANES), row), pl.BlockSpec((tm, LANES), row)]
    in_specs += [of_layer(p) for p in params]
    return pl.pallas_call(
        functools.partial(_in_proj_kernel, tiles_per_batch=tiles_per_batch, chunk=chunk, sub=min(tm, 256)),
        out_shape=out_shape, grid=(nt,), in_specs=in_specs, out_specs=out_specs,
        scratch_shapes=[pltpu.VMEM((tm + CONV_HALO, QK_COLS), F32)],
        compiler_params=_params("arbitrary"), name="in_proj",
    )(x2, cos_t, sin_t, *params)


def _mlstm_kernel(q_ref, k_ref, v_ref, gs_ref, out_ref, c_ref, m_ref, *, chunk):
    L = chunk

    @pl.when(pl.program_id(1) == 0)
    def _():
        c_ref[...] = jnp.zeros_like(c_ref)
        m_ref[...] = jnp.zeros_like(m_ref)

    b, g, big_g = gs_ref[0:SUBLANES, :], gs_ref[SUBLANES:2 * SUBLANES, :], gs_ref[2 * SUBLANES:, :]
    m_prev = jnp.concatenate([m_ref[...]] * (L // LANES), axis=1)
    big_m = jnp.maximum(big_g, m_prev)
    inter_w = jnp.exp(m_prev - big_m)
    floor = jnp.exp(-(b + big_m))
    m_end = big_m[:, L - 1:L]
    w_state = jnp.exp(g - m_end)
    m_ref[...] = jnp.broadcast_to(b[:, L - 1:L] + m_end, m_ref.shape)
    stats = [big_m, inter_w, floor, w_state]
    pad = jnp.zeros((LANES - len(stats) * SUBLANES, L), F32)
    cols = jnp.concatenate(stats + [pad], axis=0).T

    row = lax.broadcasted_iota(jnp.int32, (L, L), 0)
    col = lax.broadcasted_iota(jnp.int32, (L, L), 1)
    causal = row >= col
    lane = lax.broadcasted_iota(jnp.int32, (L, LANES), 1)
    ones_blk = jnp.where(lane == 0, 1.0, 0.0).astype(BF16)

    for h in range(ML_HEADS):
        pair, lo = h // 2, (h % 2) * ML_QK
        blk = slice(pair * LANES, (pair + 1) * LANES)
        mine = (lane >= lo) & (lane < lo + ML_QK)
        qm = jnp.where(mine, q_ref[:, blk], jnp.zeros((), BF16))
        k_blk = k_ref[:, blk]
        s = _dot_nt(qm, k_blk)
        dw = jnp.where(causal, jnp.exp(g[h:h + 1, :] - cols[:, h:h + 1]), 0.0)
        v_ext = jnp.concatenate([v_ref[:, h * ML_V:(h + 1) * ML_V], ones_blk], axis=1)
        c_pair = c_ref[pair]
        iw = cols[:, SUBLANES + h:SUBLANES + h + 1]
        num = _dot((s * dw).astype(BF16), v_ext) + iw * _dot(qm, c_pair.astype(BF16))
        den = num[:, ML_V:ML_V + 1]
        hh = num[:, :ML_V] / jnp.maximum(jnp.abs(den), cols[:, 2 * SUBLANES + h:2 * SUBLANES + h + 1])
        out_ref[:, h * ML_V:(h + 1) * ML_V] = hh.astype(out_ref.dtype)

        kw = (k_blk.astype(F32) * cols[:, 3 * SUBLANES + h:3 * SUBLANES + h + 1]).astype(BF16)
        upd = _dot_tn(kw, v_ext)
        rows = slice(lo, lo + ML_QK)
        c_ref[pair, rows, :] = inter_w[h:h + 1, L - 1:L] * c_pair[rows, :] + upd[rows, :]


def _mlstm(q, k, v, gs, *, batch, seq, chunk):
    t = q.shape[0]
    nc = seq // chunk
    row = lambda b, c: (b * nc + c, 0)
    return pl.pallas_call(
        functools.partial(_mlstm_kernel, chunk=chunk),
        out_shape=jax.ShapeDtypeStruct((t, ML_WIDTH), BF16),
        grid=(batch, nc),
        in_specs=[pl.BlockSpec((chunk, ML_QK_WIDTH), row), pl.BlockSpec((chunk, ML_QK_WIDTH), row),
                  pl.BlockSpec((chunk, ML_WIDTH), row),
                  pl.BlockSpec((3 * SUBLANES, chunk), lambda b, c: (0, b * nc + c))],
        out_specs=pl.BlockSpec((chunk, ML_WIDTH), row),
        scratch_shapes=[pltpu.VMEM((ML_HEADS // 2, LANES, 2 * LANES), F32),
                        pltpu.VMEM((SUBLANES, LANES), F32)],
        compiler_params=_params("parallel", "arbitrary"), name="mlstm",
    )(q, k, v, gs)


def _attn_kernel(q_ref, k_ref, vt_ref, out_ref, s_sc, smax_sc, m_sc, acc_sc, *, tb, hp):
    last = pl.program_id(2)
    heads = range(hp)
    ones_rows = jnp.where(lax.broadcasted_iota(jnp.int32, (16, tb), 0) == 0, 1.0, 0.0).astype(BF16)

    def produce(j, slot, h, diagonal):
        kb = k_ref[pl.ds(pl.multiple_of(j * tb, tb), tb), h * HEAD_BLOCK:(h + 1) * HEAD_BLOCK]
        s = _dot_nt(kb, q_ref[:, h * HEAD_BLOCK:(h + 1) * HEAD_BLOCK])
        if diagonal:
            key = lax.broadcasted_iota(jnp.int32, s.shape, 0)
            qry = lax.broadcasted_iota(jnp.int32, s.shape, 1)
            s = jnp.where(key <= qry, s, -jnp.inf)
        s_sc[slot, h] = s
        smax_sc[slot, h] = jnp.max(s, axis=0, keepdims=True)

    def consume(j, slot, h):
        m = m_sc[h]
        m_new = jnp.maximum(m, smax_sc[slot, h])
        p = jnp.exp2(s_sc[slot, h] - m_new).astype(BF16)
        v_ext = jnp.concatenate([vt_ref[j, h * MLA_V:(h + 1) * MLA_V, :], ones_rows], axis=0)
        acc_sc[h] = jnp.exp2(m - m_new) * acc_sc[h] + _dot(v_ext, p)
        m_sc[h] = m_new

    def stage(j, slot, next_is_diagonal):
        for h in heads:
            produce(j + 1, 1 - slot, h, next_is_diagonal)
            consume(j, slot, h)

    def finish(slot):
        for h in heads:
            consume(last, slot, h)
        o_t = jnp.concatenate([acc_sc[h, :MLA_V, :] / acc_sc[h, MLA_V:MLA_V + 1, :] for h in heads], axis=0)
        out_ref[...] = o_t.T.astype(out_ref.dtype)

    m_sc[...] = jnp.full(m_sc.shape, -jnp.inf, F32)
    acc_sc[...] = jnp.zeros(acc_sc.shape, F32)

    @pl.when(last == 0)
    def _():
        for h in heads:
            produce(0, 0, h, True)
        finish(0)

    @pl.when(last > 0)
    def _():
        for h in heads:
            produce(0, 0, h, False)

        def stages(group):
            def body(t, carry):
                for u in range(group):
                    stage(group * t + u, u % 2, False)
                return carry
            return body

        pairs = (last - 1) // 2
        lax.fori_loop(0, pairs // 4, stages(8), 0)
        lax.fori_loop(pairs // 4 * 2, pairs // 2, stages(4), 0)
        lax.fori_loop(pairs // 2 * 2, pairs, stages(2), 0)

        @pl.when(last % 2 == 1)
        def _():
            stage(last - 1, 0, True)
            finish(1)

        @pl.when(last % 2 == 0)
        def _():
            stage(last - 2, 0, False)
            stage(last - 1, 1, True)
            finish(0)


def _attention(q, k, vt, *, batch, seq, tb, hp):
    t = q.shape[0]
    nb = seq // tb
    return pl.pallas_call(
        functools.partial(_attn_kernel, tb=tb, hp=hp),
        out_shape=jax.ShapeDtypeStruct((t, MLA_WIDTH), BF16),
        grid=(batch, MLA_HEADS // hp, nb),
        in_specs=[pl.BlockSpec((tb, hp * HEAD_BLOCK), lambda b, p, i: (b * nb + i, p)),
                  pl.BlockSpec((seq, hp * HEAD_BLOCK), lambda b, p, i: (b, p)),
                  pl.BlockSpec((None, nb, hp * MLA_V, tb), lambda b, p, i: (b, 0, p, 0))],
        out_specs=pl.BlockSpec((tb, hp * MLA_V), lambda b, p, i: (b * nb + i, p)),
        scratch_shapes=[pltpu.VMEM((2, hp, tb, tb), F32), pltpu.VMEM((2, hp, 1, tb), F32),
                        pltpu.VMEM((hp, 1, tb), F32), pltpu.VMEM((hp, MLA_V + 16, tb), F32)],
        compiler_params=_params("parallel", "parallel", "arbitrary"), name="attn",
    )(q, k, vt)


def _mix_ffn_kernel(hml_ref, o_ref, hmla_ref, x_ref, hn_ref, wa_ref, wb_ref, gpost_ref, gpre_ref,
                    wu_ref, wd_ref, gmlp_ref, out_ref, m_sc, acc_sc, *, sub):
    j = pl.program_id(1)

    def mlp_slice(m):
        h = jnp.square(jnp.maximum(_dot(m, wu_ref[...]), 0.0)).astype(BF16)
        return _dot(h, wd_ref[...])

    @pl.when(j == 0)
    def _():
        blocks = [slice(r, r + sub) for r in range(0, out_ref.shape[0], sub)]
        mixes = []
        for rows in blocks:
            heads = []
            for h in range(ML_HEADS):
                hs = slice(h * ML_V, (h + 1) * ML_V)
                hh = _rms(hml_ref[rows, hs].astype(F32), hn_ref[:, hs]) * jax.nn.sigmoid(o_ref[rows, hs].astype(F32))
                heads.append(hh.astype(BF16))
            mixes.append(_dot(jnp.concatenate(heads, axis=1), wa_ref[...]) + _dot(hmla_ref[rows, :], wb_ref[...]))
        for rows, mix in zip(blocks, mixes):
            x1 = x_ref[rows, :] + _rms(mix, gpost_ref[...])
            out_ref[rows, :] = x1
            m = _rms(x1, gpre_ref[...]).astype(m_sc.dtype)
            m_sc[rows, :] = m
            acc_sc[rows, :] = mlp_slice(m)

    last = pl.num_programs(1) - 1

    @pl.when((j > 0) & (j < last))
    def _():
        acc_sc[...] += mlp_slice(m_sc[...])

    @pl.when(j == last)
    def _():
        out_ref[...] += _rms(acc_sc[...] + mlp_slice(m_sc[...]), gmlp_ref[...])


def _mix_ffn(hml, o, hmla, x2, params, layer, *, tm, tf):
    hn, wo, gpost, gpre, wu, wd, gmlp = params
    t, d = x2.shape
    dff = wu.shape[2]
    row = lambda i, j: (i, 0)
    vec = lambda arr: pl.BlockSpec((None,) + arr.shape[1:], lambda i, j: (layer, 0, 0))
    return pl.pallas_call(
        functools.partial(_mix_ffn_kernel, sub=min(tm, 256)),
        out_shape=jax.ShapeDtypeStruct((t, d), F32),
        grid=(t // tm, dff // tf),
        in_specs=[pl.BlockSpec((tm, ML_WIDTH), row), pl.BlockSpec((tm, ML_WIDTH), row),
                  pl.BlockSpec((tm, MLA_WIDTH), row), pl.BlockSpec((tm, d), row),
                  vec(hn),
                  pl.BlockSpec((None, ML_WIDTH, d), lambda i, j: (layer, 0, 0)),
                  pl.BlockSpec((None, MLA_WIDTH, d), lambda i, j: (layer, 1, 0)),
                  vec(gpost), vec(gpre),
                  pl.BlockSpec((None, d, tf), lambda i, j: (layer, 0, j)),
                  pl.BlockSpec((None, tf, d), lambda i, j: (layer, j, 0)), vec(gmlp)],
        out_specs=pl.BlockSpec((tm, d), row),
        scratch_shapes=[pltpu.VMEM((tm, d), BF16), pltpu.VMEM((tm, d), F32)],
        compiler_params=_params("parallel", "arbitrary"), name="mix_ffn",
    )(hml, o, hmla, x2, hn, wo, wo, gpost, gpre, wu, wd, gmlp)


def _layout_in_proj(w_in, b_gates):
    depth, d, _ = w_in.shape
    o0 = QK_COLS
    o1 = o0 + ML_WIDTH
    o2 = o1 + ML_WIDTH
    o3 = o2 + 2 * ML_HEADS
    o4 = o3 + MLA_Q_RANK
    o5 = o4 + MLA_KV_RANK
    gates, k_r = w_in[..., o2:o3], w_in[..., o5:]
    pad = jnp.zeros((depth, d, LANES - ROPE_LANE1), w_in.dtype)
    kr_blk = jnp.concatenate([-k_r[..., HALF_ROPE:], k_r[..., :HALF_ROPE], pad, k_r, pad], axis=2)
    wmla = jnp.concatenate([w_in[..., o3:o5], kr_blk], axis=2).astype(BF16)
    gates_t = jnp.swapaxes(gates, 1, 2)
    zrows = jnp.zeros((depth, SUBLANES - ML_HEADS, d), w_in.dtype)
    wg = jnp.concatenate([gates_t[:, :ML_HEADS], zrows, gates_t[:, ML_HEADS:], zrows], axis=1).astype(BF16)
    zb = jnp.zeros((depth, SUBLANES - ML_HEADS), b_gates.dtype)
    gbias = jnp.concatenate([b_gates[:, :ML_HEADS], zb, b_gates[:, ML_HEADS:], zb], axis=1)[..., None]
    return w_in.astype(BF16), wmla, wg, gbias


def _layout_mla(w_uq, w_ukv):
    depth, rq, _ = w_uq.shape
    rkv = w_ukv.shape[1]
    wq4 = w_uq.reshape(depth, rq, MLA_HEADS, MLA_NOPE + MLA_ROPE)
    nope, rope = wq4[..., :MLA_NOPE], wq4[..., MLA_NOPE:]
    pad = jnp.zeros((depth, rq, MLA_HEADS, HEAD_BLOCK - ROPE_LANE1 - HALF_ROPE), w_uq.dtype)
    wq = jnp.concatenate([nope, rope, rope[..., :HALF_ROPE], pad], axis=3)
    wq = wq.reshape(depth, rq, MLA_HEADS * HEAD_BLOCK).astype(BF16)
    wkv4 = w_ukv.reshape(depth, rkv, MLA_HEADS, MLA_NOPE + MLA_V)
    kpad = jnp.zeros((depth, rkv, MLA_HEADS, HEAD_BLOCK - MLA_NOPE), w_ukv.dtype)
    wk = jnp.concatenate([wkv4[..., :MLA_NOPE], kpad], axis=3).reshape(depth, rkv, MLA_HEADS * HEAD_BLOCK).astype(BF16)
    wvt = jnp.swapaxes(wkv4[..., MLA_NOPE:].reshape(depth, rkv, MLA_WIDTH), 1, 2).astype(BF16)
    return wq, wk, wvt


def kernel(x, positions, norm_pre_mix, w_in, b_gates, conv_w, conv_b, ml_head_norm, q_norm, w_uq, kv_norm, w_ukv,
           w_out, norm_post_mix, norm_pre_mlp, w_up, w_down, norm_post_mlp):
    batch, seq, d = x.shape
    depth = w_in.shape[0]
    t = batch * seq
    tm_in = min(1024, seq)
    tb = min(256, seq)
    hp = 8
    chunk = min(512, seq)
    tm_ffn = min(1024, t)
    tf = 1024

    rows = lambda v: v[:, None, :]
    win, wmla, wg, gbias = _layout_in_proj(w_in, b_gates)
    wq, wk, wvt = _layout_mla(w_uq, w_ukv)
    in_params = (rows(norm_pre_mix), win, wmla, wg, gbias, conv_w, rows(conv_b), rows(q_norm), wq, rows(kv_norm), wk, wvt)
    mix_params = (rows(ml_head_norm), w_out.astype(BF16), rows(norm_post_mix), rows(norm_pre_mlp),
                  w_up.astype(BF16), w_down.astype(BF16), rows(norm_post_mlp))

    cos_t, sin_t = _rope_tables(positions)
    x2 = x.reshape(t, d)
    for l in range(depth):
        qml, kml, v, o, gs, q, k, vt = _in_proj(x2, cos_t, sin_t, in_params, l,
                                                batch=batch, seq=seq, tm=tm_in, tk=tb, chunk=chunk)
        hml = _mlstm(qml, kml, v, gs, batch=batch, seq=seq, chunk=chunk)
        hmla = _attention(q, k, vt, batch=batch, seq=seq, tb=tb, hp=hp)
        x2 = _mix_ffn(hml, o, hmla, x2, mix_params, l, tm=tm_ffn, tf=tf)
    return x2.reshape(batch, seq, d)
```

```python
import functools

import jax
import jax.numpy as jnp
from jax import lax
from jax.experimental import pallas as pl
from jax.experimental.pallas import tpu as pltpu

F32 = jnp.float32
BF16 = jnp.bfloat16

EPS = 1e-6
ML_HEADS, ML_QK, ML_V, ML_CONV = 4, 64, 128, 4
MLA_HEADS, MLA_NOPE, MLA_ROPE, MLA_V = 8, 64, 32, 64
MLA_Q_RANK, MLA_KV_RANK = 256, 128
ROPE_THETA = 10000.0
LOG2_E = 1.4426950408889634

ML_WIDTH = ML_HEADS * ML_V
ML_QK_WIDTH = ML_HEADS * ML_QK
QK_COLS = 2 * ML_QK_WIDTH
MLA_WIDTH = MLA_HEADS * MLA_V

LANES = 128
SUBLANES = 8
HEAD_BLOCK = LANES
ROPE_LANE0 = MLA_NOPE
ROPE_LANE1 = MLA_NOPE + MLA_ROPE
HALF_ROPE = MLA_ROPE // 2
C_QK, C_V, C_O = 0, QK_COLS, QK_COLS + ML_WIDTH
ML_COLS = C_O + ML_WIDTH
C_CQ = 0
C_CKV = C_CQ + MLA_Q_RANK
C_KR = C_CKV + MLA_KV_RANK
MLA_COLS = C_KR + LANES
CONV_HALO = SUBLANES

VMEM_LIMIT = 56 * 1024 * 1024


def _params(*sem):
    return pltpu.CompilerParams(dimension_semantics=sem, vmem_limit_bytes=VMEM_LIMIT)


def _rms(x, g):
    return x * lax.rsqrt(jnp.mean(x * x, axis=-1, keepdims=True) + EPS) * g


def _log_sigmoid(x):
    return -(jnp.maximum(-x, 0.0) + jnp.log1p(jnp.exp(-jnp.abs(x))))


def _dot(a, b):
    return jnp.dot(a, b, preferred_element_type=F32)


def _dot_nt(a, b):
    return lax.dot_general(a, b, (((1,), (1,)), ((), ())), preferred_element_type=F32)


def _dot_tn(a, b):
    return lax.dot_general(a, b, (((0,), (0,)), ((), ())), preferred_element_type=F32)


def _rope_table_kernel(pos_ref, inv_ref, cos_ref, sin_ref):
    pos = pos_ref[...].astype(F32)
    slot = lax.broadcasted_iota(jnp.int32, cos_ref.shape, 1) // HALF_ROPE
    ang = jnp.zeros(cos_ref.shape, F32)
    for t in range(pos.shape[1]):
        ang = jnp.where(slot == t, pos[:, t:t + 1], ang)
    ang = ang * inv_ref[...]
    cos_ref[...] = jnp.cos(ang)
    sin_ref[...] = jnp.sin(ang)


def _rope_tables(positions):
    t = positions.size
    per_row = LANES // HALF_ROPE
    rows = t // per_row
    tr = min(rows, 512)
    inv = 1.0 / (ROPE_THETA ** (jnp.arange(0, MLA_ROPE, 2, dtype=F32) / MLA_ROPE))
    cos_c, sin_c = pl.pallas_call(
        _rope_table_kernel,
        out_shape=(jax.ShapeDtypeStruct((rows, LANES), F32),) * 2,
        grid=(rows // tr,),
        in_specs=[pl.BlockSpec((tr, per_row), lambda i: (i, 0)),
                  pl.BlockSpec((1, LANES), lambda i: (0, 0))],
        out_specs=(pl.BlockSpec((tr, LANES), lambda i: (i, 0)),) * 2,
        compiler_params=_params("parallel"),
        name="rope_tables",
    )(positions.reshape(rows, per_row), jnp.tile(inv, per_row).reshape(1, LANES))
    cos_h, sin_h = cos_c.reshape(t, HALF_ROPE), sin_c.reshape(t, HALF_ROPE)
    ones = jnp.ones((t, ROPE_LANE0), F32)
    zeros = jnp.zeros((t, ROPE_LANE0), F32)
    pad = jnp.zeros((t, LANES - ROPE_LANE1), F32)
    return (jnp.concatenate([ones, cos_h, cos_h, pad], axis=1), jnp.concatenate([zeros, sin_h, sin_h, pad], axis=1))


def _scan_lanes(x, op, identity, segment):
    pos = lax.broadcasted_iota(jnp.int32, x.shape, 1) % segment
    step = 1
    while step < segment:
        x = op(x, jnp.where(pos >= step, pltpu.roll(x, step, axis=1), identity))
        step *= 2
    return x


def _in_proj_kernel(x_ref, cos_ref, sin_ref, g_ref, win_ref, wmla_ref, wg_ref, gbias_ref, cw_ref, cb_ref, qn_ref, wq_ref,
                    kvn_ref, wk_ref, wvt_ref,
                    qml_ref, kml_ref, v_ref, o_ref, gs_ref, q_ref, k_ref, vt_ref, xbuf,
                    *, tiles_per_batch, chunk, sub):
    tm = x_ref.shape[0]
    scale = (MLA_NOPE + MLA_ROPE) ** -0.5 * LOG2_E

    @pl.when(pl.program_id(0) % tiles_per_batch == 0)
    def _():
        xbuf[0:CONV_HALO, :] = jnp.zeros((CONV_HALO, QK_COLS), F32)

    blocks = [slice(r, r + sub) for r in range(0, tm, sub)]
    acts = [_rms(x_ref[rows, :], g_ref[...]).astype(BF16) for rows in blocks]
    projs = [_dot(a, win_ref[...]) for a in acts]
    projs_mla = [_dot(a, wmla_ref[...]) for a in acts]

    gates = jnp.concatenate([_dot_nt(wg_ref[...], a) for a in acts], axis=1) + gbias_ref[...]
    b = _scan_lanes(_log_sigmoid(gates[SUBLANES:]), jnp.add, 0.0, chunk)
    g = gates[:SUBLANES] - b
    gs_ref[...] = jnp.concatenate([b, g, _scan_lanes(g, jnp.maximum, -jnp.inf, chunk)], axis=0)

    for rows, proj, pmla in zip(blocks, projs, projs_mla):
        v_ref[rows, :] = proj[:, C_V:C_O].astype(BF16)
        o_ref[rows, :] = proj[:, C_O:ML_COLS].astype(BF16)

        top = CONV_HALO + rows.start
        xbuf[top:top + sub, :] = proj[:, C_QK:C_V]
        cw = cw_ref[...]
        y = cb_ref[...]
        for tap in range(ML_CONV):
            start = top - (ML_CONV - 1 - tap)
            y = y + xbuf[start:start + sub, :] * cw[tap:tap + 1, :]
        qk = y * jax.nn.sigmoid(y)
        qml_ref[rows, :] = qk[:, :ML_QK_WIDTH].astype(BF16)
        kml_ref[rows, :] = (qk[:, ML_QK_WIDTH:] * (ML_QK ** -0.5)).astype(BF16)

        cos_t = cos_ref[rows, :]
        lane = lax.broadcasted_iota(jnp.int32, cos_t.shape, 1)
        first_half = lane < ROPE_LANE0 + HALF_ROPE
        sin_t = sin_ref[rows, :]
        sin_q = jnp.where(first_half, -sin_t, sin_t)
        qn = _rms(pmla[:, C_CQ:C_CKV], qn_ref[...]).astype(BF16)
        qf = _dot(qn, wq_ref[...])
        ckvn = _rms(pmla[:, C_CKV:C_KR], kvn_ref[...]).astype(BF16)
        kf = _dot(ckvn, wk_ref[...])
        kr = pmla[:, C_KR:MLA_COLS]
        rope = (lane >= ROPE_LANE0) & (lane < ROPE_LANE1)
        k_rope = jnp.where(rope, kr * cos_t + pltpu.roll(kr, LANES // 2, axis=1) * sin_t, 0.0)
        for h in range(MLA_HEADS):
            blk = slice(h * HEAD_BLOCK, (h + 1) * HEAD_BLOCK)
            qh = qf[:, blk]
            swapped = pltpu.roll(qh, LANES - HALF_ROPE, axis=1)
            q_ref[rows, blk] = ((qh * cos_t + swapped * sin_q) * scale).astype(BF16)
            k_ref[rows, blk] = (kf[:, blk] + k_rope).astype(BF16)

        vt = _dot_nt(wvt_ref[...], ckvn).astype(BF16)
        if len(vt_ref.shape) == 2:
            vt_ref[:, rows] = vt
        else:
            tk = vt_ref.shape[-1]
            for j in range(sub // tk):
                vt_ref[rows.start // tk + j] = vt[:, j * tk:(j + 1) * tk]

    xbuf[0:CONV_HALO, :] = xbuf[tm:tm + CONV_HALO, :]


def _in_proj(x2, cos_t, sin_t, params, layer, *, batch, seq, tm, tk, chunk):
    t, d = x2.shape
    nt = t // tm
    tiles_per_batch = seq // tm
    if tk >= tm:
        per_kv = tk // tm
        vt_spec = pl.BlockSpec((None, None, MLA_WIDTH, tm),
                               lambda i: (i // tiles_per_batch, (i % tiles_per_batch) // per_kv, 0, i % per_kv))
    else:
        vt_spec = pl.BlockSpec((None, tm // tk, MLA_WIDTH, tk),
                               lambda i: (i // tiles_per_batch, i % tiles_per_batch, 0, 0))
    row = lambda i: (i, 0)
    col = lambda i: (0, i)
    of_layer = lambda arr: pl.BlockSpec((None,) + arr.shape[1:], lambda i: (layer, 0, 0))
    out_shape = (
        jax.ShapeDtypeStruct((t, ML_QK_WIDTH), BF16),
        jax.ShapeDtypeStruct((t, ML_QK_WIDTH), BF16),
        jax.ShapeDtypeStruct((t, ML_WIDTH), BF16),
        jax.ShapeDtypeStruct((t, ML_WIDTH), BF16),
        jax.ShapeDtypeStruct((3 * SUBLANES, t), F32),
        jax.ShapeDtypeStruct((t, MLA_HEADS * HEAD_BLOCK), BF16),
        jax.ShapeDtypeStruct((t, MLA_HEADS * HEAD_BLOCK), BF16),
        jax.ShapeDtypeStruct((batch, seq // tk, MLA_WIDTH, tk), BF16),
    )
    out_specs = (
        pl.BlockSpec((tm, ML_QK_WIDTH), row),
        pl.BlockSpec((tm, ML_QK_WIDTH), row),
        pl.BlockSpec((tm, ML_WIDTH), row),
        pl.BlockSpec((tm, ML_WIDTH), row),
        pl.BlockSpec((3 * SUBLANES, tm), col),
        pl.BlockSpec((tm, MLA_HEADS * HEAD_BLOCK), row),
        pl.BlockSpec((tm, MLA_HEADS * HEAD_BLOCK), row),
        vt_spec,
    )
    in_specs = [pl.BlockSpec((tm, d), row), pl.BlockSpec((tm, LANES), row), pl.BlockSpec((tm, LANES), row)]
    in_specs += [of_layer(p) for p in params]
    return pl.pallas_call(
        functools.partial(_in_proj_kernel, tiles_per_batch=tiles_per_batch, chunk=chunk, sub=min(tm, 256)),
        out_shape=out_shape, grid=(nt,), in_specs=in_specs, out_specs=out_specs,
        scratch_shapes=[pltpu.VMEM((tm + CONV_HALO, QK_COLS), F32)],
        compiler_params=_params("arbitrary"), name="in_proj",
    )(x2, cos_t, sin_t, *params)


def _mlstm_kernel(q_ref, k_ref, v_ref, gs_ref, out_ref, c_ref, m_ref, *, chunk):
    L = chunk

    @pl.when(pl.program_id(1) == 0)
    def _():
        c_ref[...] = jnp.zeros_like(c_ref)
        m_ref[...] = jnp.zeros_like(m_ref)

    b, g, big_g = gs_ref[0:SUBLANES, :], gs_ref[SUBLANES:2 * SUBLANES, :], gs_ref[2 * SUBLANES:, :]
    m_prev = jnp.concatenate([m_ref[...]] * (L // LANES), axis=1)
    big_m = jnp.maximum(big_g, m_prev)
    inter_w = jnp.exp(m_prev - big_m)
    floor = jnp.exp(-(b + big_m))
    m_end = big_m[:, L - 1:L]
    w_state = jnp.exp(g - m_end)
    m_ref[...] = jnp.broadcast_to(b[:, L - 1:L] + m_end, m_ref.shape)
    stats = [big_m, inter_w, floor, w_state]
    pad = jnp.zeros((LANES - len(stats) * SUBLANES, L), F32)
    cols = jnp.concatenate(stats + [pad], axis=0).T

    lane = lax.broadcasted_iota(jnp.int32, (L, LANES), 1)
    ones_blk = jnp.where(lane == 0, 1.0, 0.0).astype(BF16)
    rb = min(L, 256)
    row_blocks = [(slice(r, r + rb), slice(0, r + rb)) for r in range(0, L, rb)]

    for h in range(ML_HEADS):
        pair, lo = h // 2, (h % 2) * ML_QK
        blk = slice(pair * LANES, (pair + 1) * LANES)
        mine = (lane >= lo) & (lane < lo + ML_QK)
        qm = jnp.where(mine, q_ref[:, blk], jnp.zeros((), BF16))
        k_blk = k_ref[:, blk]
        v_ext = jnp.concatenate([v_ref[:, h * ML_V:(h + 1) * ML_V], ones_blk], axis=1)
        c_pair = c_ref[pair]
        c_bf = c_pair.astype(BF16)
        for rows, keys in row_blocks:
            n_keys = keys.stop
            s = _dot_nt(qm[rows], k_blk[keys])
            qpos = rows.start + lax.broadcasted_iota(jnp.int32, (rb, n_keys), 0)
            kpos = lax.broadcasted_iota(jnp.int32, (rb, n_keys), 1)
            dw = jnp.where(qpos >= kpos, jnp.exp(g[h:h + 1, keys] - cols[rows, h:h + 1]), 0.0)
            iw = cols[rows, SUBLANES + h:SUBLANES + h + 1]
            num = _dot((s * dw).astype(BF16), v_ext[keys]) + iw * _dot(qm[rows], c_bf)
            den = num[:, ML_V:ML_V + 1]
            hh = num[:, :ML_V] / jnp.maximum(jnp.abs(den), cols[rows, 2 * SUBLANES + h:2 * SUBLANES + h + 1])
            out_ref[rows, h * ML_V:(h + 1) * ML_V] = hh.astype(out_ref.dtype)

        kw = (k_blk.astype(F32) * cols[:, 3 * SUBLANES + h:3 * SUBLANES + h + 1]).astype(BF16)
        upd = _dot_tn(kw, v_ext)
        rows = slice(lo, lo + ML_QK)
        c_ref[pair, rows, :] = inter_w[h:h + 1, L - 1:L] * c_pair[rows, :] + upd[rows, :]


def _mlstm(q, k, v, gs, *, batch, seq, chunk):
    t = q.shape[0]
    nc = seq // chunk
    row = lambda b, c: (b * nc + c, 0)
    return pl.pallas_call(
        functools.partial(_mlstm_kernel, chunk=chunk),
        out_shape=jax.ShapeDtypeStruct((t, ML_WIDTH), BF16),
        grid=(batch, nc),
        in_specs=[pl.BlockSpec((chunk, ML_QK_WIDTH), row), pl.BlockSpec((chunk, ML_QK_WIDTH), row),
                  pl.BlockSpec((chunk, ML_WIDTH), row),
                  pl.BlockSpec((3 * SUBLANES, chunk), lambda b, c: (0, b * nc + c))],
        out_specs=pl.BlockSpec((chunk, ML_WIDTH), row),
        scratch_shapes=[pltpu.VMEM((ML_HEADS // 2, LANES, 2 * LANES), F32),
                        pltpu.VMEM((SUBLANES, LANES), F32)],
        compiler_params=_params("parallel", "arbitrary"), name="mlstm",
    )(q, k, v, gs)


def _attn_kernel(q_ref, k_ref, vt_ref, out_ref, s_sc, smax_sc, m_sc, acc_sc, *, tb, hp):
    last = pl.program_id(2)
    heads = range(hp)
    ones_rows = jnp.where(lax.broadcasted_iota(jnp.int32, (16, tb), 0) == 0, 1.0, 0.0).astype(BF16)

    def produce(j, slot, h, diagonal):
        kb = k_ref[pl.ds(pl.multiple_of(j * tb, tb), tb), h * HEAD_BLOCK:(h + 1) * HEAD_BLOCK]
        s = _dot_nt(kb, q_ref[:, h * HEAD_BLOCK:(h + 1) * HEAD_BLOCK])
        if diagonal:
            key = lax.broadcasted_iota(jnp.int32, s.shape, 0)
            qry = lax.broadcasted_iota(jnp.int32, s.shape, 1)
            s = jnp.where(key <= qry, s, -jnp.inf)
        s_sc[slot, h] = s
        smax_sc[slot, h] = jnp.max(s, axis=0, keepdims=True)

    def consume(j, slot, h):
        m = m_sc[h]
        m_new = jnp.maximum(m, smax_sc[slot, h])
        p = jnp.exp2(s_sc[slot, h] - m_new).astype(BF16)
        v_ext = jnp.concatenate([vt_ref[j, h * MLA_V:(h + 1) * MLA_V, :], ones_rows], axis=0)
        acc_sc[h] = jnp.exp2(m - m_new) * acc_sc[h] + _dot(v_ext, p)
        m_sc[h] = m_new

    def stage(j, slot, next_is_diagonal):
        for h in heads:
            produce(j + 1, 1 - slot, h, next_is_diagonal)
            consume(j, slot, h)

    def finish(slot):
        for h in heads:
            consume(last, slot, h)
        o_t = jnp.concatenate([acc_sc[h, :MLA_V, :] / acc_sc[h, MLA_V:MLA_V + 1, :] for h in heads], axis=0)
        out_ref[...] = o_t.T.astype(out_ref.dtype)

    m_sc[...] = jnp.full(m_sc.shape, -jnp.inf, F32)
    acc_sc[...] = jnp.zeros(acc_sc.shape, F32)

    @pl.when(last == 0)
    def _():
        for h in heads:
            produce(0, 0, h, True)
        finish(0)

    @pl.when(last > 0)
    def _():
        for h in heads:
            produce(0, 0, h, False)

        def stages(group):
            def body(t, carry):
                for u in range(group):
                    stage(group * t + u, u % 2, False)
                return carry
            return body

        pairs = (last - 1) // 2
        lax.fori_loop(0, pairs // 4, stages(8), 0)
        lax.fori_loop(pairs // 4 * 2, pairs // 2, stages(4), 0)
        lax.fori_loop(pairs // 2 * 2, pairs, stages(2), 0)

        @pl.when(last % 2 == 1)
        def _():
            stage(last - 1, 0, True)
            finish(1)

        @pl.when(last % 2 == 0)
        def _():
            stage(last - 2, 0, False)
            stage(last - 1, 1, True)
            finish(0)


def _attention(q, k, vt, *, batch, seq, tb, hp):
    t = q.shape[0]
    nb = seq // tb
    return pl.pallas_call(
        functools.partial(_attn_kernel, tb=tb, hp=hp),
        out_shape=jax.ShapeDtypeStruct((t, MLA_WIDTH), BF16),
        grid=(batch, MLA_HEADS // hp, nb),
        in_specs=[pl.BlockSpec((tb, hp * HEAD_BLOCK), lambda b, p, i: (b * nb + i, p)),
                  pl.BlockSpec((seq, hp * HEAD_BLOCK), lambda b, p, i: (b, p)),
                  pl.BlockSpec((None, nb, hp * MLA_V, tb), lambda b, p, i: (b, 0, p, 0))],
        out_specs=pl.BlockSpec((tb, hp * MLA_V), lambda b, p, i: (b * nb + i, p)),
        scratch_shapes=[pltpu.VMEM((2, hp, tb, tb), F32), pltpu.VMEM((2, hp, 1, tb), F32),
                        pltpu.VMEM((hp, 1, tb), F32), pltpu.VMEM((hp, MLA_V + 16, tb), F32)],
        compiler_params=_params("parallel", "parallel", "arbitrary"), name="attn",
    )(q, k, vt)


def _mix_ffn_kernel(hml_ref, o_ref, hmla_ref, x_ref, hn_ref, wa_ref, wb_ref, gpost_ref, gpre_ref,
                    wu_ref, wd_ref, gmlp_ref, out_ref, m_sc, acc_sc, *, sub):
    j = pl.program_id(1)

    def mlp_slice(m):
        h = jnp.square(jnp.maximum(_dot(m, wu_ref[...]), 0.0)).astype(BF16)
        return _dot(h, wd_ref[...])

    @pl.when(j == 0)
    def _():
        blocks = [slice(r, r + sub) for r in range(0, out_ref.shape[0], sub)]
        mixes = []
        for rows in blocks:
            heads = []
            for h in range(ML_HEADS):
                hs = slice(h * ML_V, (h + 1) * ML_V)
                hh = _rms(hml_ref[rows, hs].astype(F32), hn_ref[:, hs]) * jax.nn.sigmoid(o_ref[rows, hs].astype(F32))
                heads.append(hh.astype(BF16))
            mixes.append(_dot(jnp.concatenate(heads, axis=1), wa_ref[...]) + _dot(hmla_ref[rows, :], wb_ref[...]))
        for rows, mix in zip(blocks, mixes):
            x1 = x_ref[rows, :] + _rms(mix, gpost_ref[...])
            out_ref[rows, :] = x1
            m = _rms(x1, gpre_ref[...]).astype(m_sc.dtype)
            m_sc[rows, :] = m
            acc_sc[rows, :] = mlp_slice(m)

    last = pl.num_programs(1) - 1

    @pl.when((j > 0) & (j < last))
    def _():
        acc_sc[...] += mlp_slice(m_sc[...])

    @pl.when(j == last)
    def _():
        out_ref[...] += _rms(acc_sc[...] + mlp_slice(m_sc[...]), gmlp_ref[...])


def _mix_ffn(hml, o, hmla, x2, params, layer, *, tm, tf):
    hn, wo, gpost, gpre, wu, wd, gmlp = params
    t, d = x2.shape
    dff = wu.shape[2]
    row = lambda i, j: (i, 0)
    vec = lambda arr: pl.BlockSpec((None,) + arr.shape[1:], lambda i, j: (layer, 0, 0))
    return pl.pallas_call(
        functools.partial(_mix_ffn_kernel, sub=min(tm, 256)),
        out_shape=jax.ShapeDtypeStruct((t, d), F32),
        grid=(t // tm, dff // tf),
        in_specs=[pl.BlockSpec((tm, ML_WIDTH), row), pl.BlockSpec((tm, ML_WIDTH), row),
                  pl.BlockSpec((tm, MLA_WIDTH), row), pl.BlockSpec((tm, d), row),
                  vec(hn),
                  pl.BlockSpec((None, ML_WIDTH, d), lambda i, j: (layer, 0, 0)),
                  pl.BlockSpec((None, MLA_WIDTH, d), lambda i, j: (layer, 1, 0)),
                  vec(gpost), vec(gpre),
                  pl.BlockSpec((None, d, tf), lambda i, j: (layer, 0, j)),
                  pl.BlockSpec((None, tf, d), lambda i, j: (layer, j, 0)), vec(gmlp)],
        out_specs=pl.BlockSpec((tm, d), row),
        scratch_shapes=[pltpu.VMEM((tm, d), BF16), pltpu.VMEM((tm, d), F32)],
        compiler_params=_params("parallel", "arbitrary"), name="mix_ffn",
    )(hml, o, hmla, x2, hn, wo, wo, gpost, gpre, wu, wd, gmlp)


def _layout_in_proj(w_in, b_gates):
    depth, d, _ = w_in.shape
    o0 = QK_COLS
    o1 = o0 + ML_WIDTH
    o2 = o1 + ML_WIDTH
    o3 = o2 + 2 * ML_HEADS
    o4 = o3 + MLA_Q_RANK
    o5 = o4 + MLA_KV_RANK
    gates, k_r = w_in[..., o2:o3], w_in[..., o5:]
    pad = jnp.zeros((depth, d, LANES - ROPE_LANE1), w_in.dtype)
    kr_blk = jnp.concatenate([-k_r[..., HALF_ROPE:], k_r[..., :HALF_ROPE], pad, k_r, pad], axis=2)
    wmla = jnp.concatenate([w_in[..., o3:o5], kr_blk], axis=2).astype(BF16)
    gates_t = jnp.swapaxes(gates, 1, 2)
    zrows = jnp.zeros((depth, SUBLANES - ML_HEADS, d), w_in.dtype)
    wg = jnp.concatenate([gates_t[:, :ML_HEADS], zrows, gates_t[:, ML_HEADS:], zrows], axis=1).astype(BF16)
    zb = jnp.zeros((depth, SUBLANES - ML_HEADS), b_gates.dtype)
    gbias = jnp.concatenate([b_gates[:, :ML_HEADS], zb, b_gates[:, ML_HEADS:], zb], axis=1)[..., None]
    return w_in[..., :ML_COLS].astype(BF16), wmla, wg, gbias


def _layout_mla(w_uq, w_ukv):
    depth, rq, _ = w_uq.shape
    rkv = w_ukv.shape[1]
    wq4 = w_uq.reshape(depth, rq, MLA_HEADS, MLA_NOPE + MLA_ROPE)
    nope, rope = wq4[..., :MLA_NOPE], wq4[..., MLA_NOPE:]
    pad = jnp.zeros((depth, rq, MLA_HEADS, HEAD_BLOCK - ROPE_LANE1 - HALF_ROPE), w_uq.dtype)
    wq = jnp.concatenate([nope, rope, rope[..., :HALF_ROPE], pad], axis=3)
    wq = wq.reshape(depth, rq, MLA_HEADS * HEAD_BLOCK).astype(BF16)
    wkv4 = w_ukv.reshape(depth, rkv, MLA_HEADS, MLA_NOPE + MLA_V)
    kpad = jnp.zeros((depth, rkv, MLA_HEADS, HEAD_BLOCK - MLA_NOPE), w_ukv.dtype)
    wk = jnp.concatenate([wkv4[..., :MLA_NOPE], kpad], axis=3).reshape(depth, rkv, MLA_HEADS * HEAD_BLOCK).astype(BF16)
    wvt = jnp.swapaxes(wkv4[..., MLA_NOPE:].reshape(depth, rkv, MLA_WIDTH), 1, 2).astype(BF16)
    return wq, wk, wvt


def kernel(x, positions, norm_pre_mix, w_in, b_gates, conv_w, conv_b, ml_head_norm, q_norm, w_uq, kv_norm, w_ukv,
           w_out, norm_post_mix, norm_pre_mlp, w_up, w_down, norm_post_mlp):
    batch, seq, d = x.shape
    depth = w_in.shape[0]
    t = batch * seq
    tm_in = min(1024, seq)
    tb = min(256, seq)
    hp = 8
    chunk = min(512, seq)
    tm_ffn = min(1024, t)
    tf = 1024

    rows = lambda v: v[:, None, :]
    win, wmla, wg, gbias = _layout_in_proj(w_in, b_gates)
    wq, wk, wvt = _layout_mla(w_uq, w_ukv)
    in_params = (rows(norm_pre_mix), win, wmla, wg, gbias, conv_w, rows(conv_b), rows(q_norm), wq, rows(kv_norm), wk, wvt)
    mix_params = (rows(ml_head_norm), w_out.astype(BF16), rows(norm_post_mix), rows(norm_pre_mlp),
                  w_up.astype(BF16), w_down.astype(BF16), rows(norm_post_mlp))

    cos_t, sin_t = _rope_tables(positions)
    x2 = x.reshape(t, d)
    for l in range(depth):
        qml, kml, v, o, gs, q, k, vt = _in_proj(x2, cos_t, sin_t, in_params, l,
                                                batch=batch, seq=seq, tm=tm_in, tk=tb, chunk=chunk)
        hml = _mlstm(qml, kml, v, gs, batch=batch, seq=seq, chunk=chunk)
        hmla = _attention(q, k, vt, batch=batch, seq=seq, tb=tb, hp=hp)
        x2 = _mix_ffn(hml, o, hmla, x2, mix_params, l, tm=tm_ffn, tf=tf)
    return x2.reshape(batch, seq, d)
```

```python
import functools

import jax
import jax.numpy as jnp
from jax import lax
from jax.experimental import pallas as pl
from jax.experimental.pallas import tpu as pltpu

F32 = jnp.float32
BF16 = jnp.bfloat16

EPS = 1e-6
ML_HEADS, ML_QK, ML_V, ML_CONV = 4, 64, 128, 4
MLA_HEADS, MLA_NOPE, MLA_ROPE, MLA_V = 8, 64, 32, 64
MLA_Q_RANK, MLA_KV_RANK = 256, 128
ROPE_THETA = 10000.0
LOG2_E = 1.4426950408889634

ML_WIDTH = ML_HEADS * ML_V
ML_QK_WIDTH = ML_HEADS * ML_QK
QK_COLS = 2 * ML_QK_WIDTH
MLA_WIDTH = MLA_HEADS * MLA_V

LANES = 128
SUBLANES = 8
HEAD_BLOCK = LANES
ROPE_LANE0 = MLA_NOPE
ROPE_LANE1 = MLA_NOPE + MLA_ROPE
HALF_ROPE = MLA_ROPE // 2
C_QK, C_V, C_O = 0, QK_COLS, QK_COLS + ML_WIDTH
ML_COLS = C_O + ML_WIDTH
C_CQ = 0
C_CKV = C_CQ + MLA_Q_RANK
C_KR = C_CKV + MLA_KV_RANK
MLA_COLS = C_KR + LANES
CONV_HALO = SUBLANES

VMEM_LIMIT = 56 * 1024 * 1024


def _params(*sem):
    return pltpu.CompilerParams(dimension_semantics=sem, vmem_limit_bytes=VMEM_LIMIT)


def _rms(x, g):
    return x * lax.rsqrt(jnp.mean(x * x, axis=-1, keepdims=True) + EPS) * g


def _log_sigmoid(x):
    return -(jnp.maximum(-x, 0.0) + jnp.log1p(jnp.exp(-jnp.abs(x))))


def _dot(a, b):
    return jnp.dot(a, b, preferred_element_type=F32)


def _dot_nt(a, b):
    return lax.dot_general(a, b, (((1,), (1,)), ((), ())), preferred_element_type=F32)


def _dot_tn(a, b):
    return lax.dot_general(a, b, (((0,), (0,)), ((), ())), preferred_element_type=F32)


def _rope_table_kernel(pos_ref, inv_ref, cos_ref, sin_ref):
    pos = pos_ref[...].astype(F32)
    slot = lax.broadcasted_iota(jnp.int32, cos_ref.shape, 1) // HALF_ROPE
    ang = jnp.zeros(cos_ref.shape, F32)
    for t in range(pos.shape[1]):
        ang = jnp.where(slot == t, pos[:, t:t + 1], ang)
    ang = ang * inv_ref[...]
    cos_ref[...] = jnp.cos(ang)
    sin_ref[...] = jnp.sin(ang)


def _rope_tables(positions):
    t = positions.size
    per_row = LANES // HALF_ROPE
    rows = t // per_row
    tr = min(rows, 512)
    inv = 1.0 / (ROPE_THETA ** (jnp.arange(0, MLA_ROPE, 2, dtype=F32) / MLA_ROPE))
    cos_c, sin_c = pl.pallas_call(
        _rope_table_kernel,
        out_shape=(jax.ShapeDtypeStruct((rows, LANES), F32),) * 2,
        grid=(rows // tr,),
        in_specs=[pl.BlockSpec((tr, per_row), lambda i: (i, 0)),
                  pl.BlockSpec((1, LANES), lambda i: (0, 0))],
        out_specs=(pl.BlockSpec((tr, LANES), lambda i: (i, 0)),) * 2,
        compiler_params=_params("parallel"),
        name="rope_tables",
    )(positions.reshape(rows, per_row), jnp.tile(inv, per_row).reshape(1, LANES))
    cos_h, sin_h = cos_c.reshape(t, HALF_ROPE), sin_c.reshape(t, HALF_ROPE)
    ones = jnp.ones((t, ROPE_LANE0), F32)
    zeros = jnp.zeros((t, ROPE_LANE0), F32)
    pad = jnp.zeros((t, LANES - ROPE_LANE1), F32)
    return (jnp.concatenate([ones, cos_h, cos_h, pad], axis=1), jnp.concatenate([zeros, sin_h, sin_h, pad], axis=1))


def _scan_lanes(x, op, identity, segment):
    pos = lax.broadcasted_iota(jnp.int32, x.shape, 1) % segment
    step = 1
    while step < segment:
        x = op(x, jnp.where(pos >= step, pltpu.roll(x, step, axis=1), identity))
        step *= 2
    return x


def _in_proj_kernel(x_ref, cos_ref, sin_ref, g_ref, win_ref, wmla_ref, wg_ref, gbias_ref, cw_ref, cb_ref, qn_ref, wq_ref,
                    kvn_ref, wk_ref, wvt_ref,
                    qml_ref, kml_ref, v_ref, o_ref, gs_ref, q_ref, k_ref, vt_ref, xbuf,
                    *, tiles_per_batch, chunk, sub):
    tm = x_ref.shape[0]
    scale = (MLA_NOPE + MLA_ROPE) ** -0.5 * LOG2_E

    @pl.when(pl.program_id(0) % tiles_per_batch == 0)
    def _():
        xbuf[0:CONV_HALO, :] = jnp.zeros((CONV_HALO, QK_COLS), F32)

    blocks = [slice(r, r + sub) for r in range(0, tm, sub)]
    acts = [_rms(x_ref[rows, :], g_ref[...]).astype(BF16) for rows in blocks]
    projs = [_dot(a, win_ref[...]) for a in acts]
    projs_mla = [_dot(a, wmla_ref[...]) for a in acts]

    gates = jnp.concatenate([_dot_nt(wg_ref[...], a) for a in acts], axis=1) + gbias_ref[...]
    b = _scan_lanes(_log_sigmoid(gates[SUBLANES:]), jnp.add, 0.0, chunk)
    g = gates[:SUBLANES] - b
    gs_ref[...] = jnp.concatenate([b, g, _scan_lanes(g, jnp.maximum, -jnp.inf, chunk)], axis=0)

    for rows, proj, pmla in zip(blocks, projs, projs_mla):
        v_ref[rows, :] = proj[:, C_V:C_O].astype(BF16)
        o_ref[rows, :] = proj[:, C_O:ML_COLS].astype(BF16)

        top = CONV_HALO + rows.start
        xbuf[top:top + sub, :] = proj[:, C_QK:C_V]
        cw = cw_ref[...]
        y = cb_ref[...]
        for tap in range(ML_CONV):
            start = top - (ML_CONV - 1 - tap)
            y = y + xbuf[start:start + sub, :] * cw[tap:tap + 1, :]
        qk = y * jax.nn.sigmoid(y)
        qml_ref[rows, :] = qk[:, :ML_QK_WIDTH].astype(BF16)
        kml_ref[rows, :] = (qk[:, ML_QK_WIDTH:] * (ML_QK ** -0.5)).astype(BF16)

        cos_t = cos_ref[rows, :]
        lane = lax.broadcasted_iota(jnp.int32, cos_t.shape, 1)
        first_half = lane < ROPE_LANE0 + HALF_ROPE
        sin_t = sin_ref[rows, :]
        sin_q = jnp.where(first_half, -sin_t, sin_t)
        qn = _rms(pmla[:, C_CQ:C_CKV], qn_ref[...]).astype(BF16)
        qf = _dot(qn, wq_ref[...])
        ckvn = _rms(pmla[:, C_CKV:C_KR], kvn_ref[...]).astype(BF16)
        kf = _dot(ckvn, wk_ref[...])
        kr = pmla[:, C_KR:MLA_COLS]
        rope = (lane >= ROPE_LANE0) & (lane < ROPE_LANE1)
        k_rope = jnp.where(rope, kr * cos_t + pltpu.roll(kr, LANES // 2, axis=1) * sin_t, 0.0)
        for h in range(MLA_HEADS):
            blk = slice(h * HEAD_BLOCK, (h + 1) * HEAD_BLOCK)
            qh = qf[:, blk]
            swapped = pltpu.roll(qh, LANES - HALF_ROPE, axis=1)
            q_ref[rows, blk] = ((qh * cos_t + swapped * sin_q) * scale).astype(BF16)
            k_ref[rows, blk] = (kf[:, blk] + k_rope).astype(BF16)

        vt = _dot_nt(wvt_ref[...], ckvn).astype(BF16)
        if len(vt_ref.shape) == 2:
            vt_ref[:, rows] = vt
        else:
            tk = vt_ref.shape[-1]
            piece = min(sub, tk)
            for j in range(sub // piece):
                start = rows.start + j * piece
                vt_ref[start // tk, :, start % tk:start % tk + piece] = vt[:, j * piece:(j + 1) * piece]

    xbuf[0:CONV_HALO, :] = xbuf[tm:tm + CONV_HALO, :]


def _in_proj(x2, cos_t, sin_t, params, layer, *, batch, seq, tm, tk, chunk):
    t, d = x2.shape
    nt = t // tm
    tiles_per_batch = seq // tm
    if tk >= tm:
        per_kv = tk // tm
        vt_spec = pl.BlockSpec((None, None, MLA_WIDTH, tm),
                               lambda i: (i // tiles_per_batch, (i % tiles_per_batch) // per_kv, 0, i % per_kv))
    else:
        vt_spec = pl.BlockSpec((None, tm // tk, MLA_WIDTH, tk),
                               lambda i: (i // tiles_per_batch, i % tiles_per_batch, 0, 0))
    row = lambda i: (i, 0)
    col = lambda i: (0, i)
    of_layer = lambda arr: pl.BlockSpec((None,) + arr.shape[1:], lambda i: (layer, 0, 0))
    out_shape = (
        jax.ShapeDtypeStruct((t, ML_QK_WIDTH), BF16),
        jax.ShapeDtypeStruct((t, ML_QK_WIDTH), BF16),
        jax.ShapeDtypeStruct((t, ML_WIDTH), BF16),
        jax.ShapeDtypeStruct((t, ML_WIDTH), BF16),
        jax.ShapeDtypeStruct((3 * SUBLANES, t), F32),
        jax.ShapeDtypeStruct((t, MLA_HEADS * HEAD_BLOCK), BF16),
        jax.ShapeDtypeStruct((t, MLA_HEADS * HEAD_BLOCK), BF16),
        jax.ShapeDtypeStruct((batch, seq // tk, MLA_WIDTH, tk), BF16),
    )
    out_specs = (
        pl.BlockSpec((tm, ML_QK_WIDTH), row),
        pl.BlockSpec((tm, ML_QK_WIDTH), row),
        pl.BlockSpec((tm, ML_WIDTH), row),
        pl.BlockSpec((tm, ML_WIDTH), row),
        pl.BlockSpec((3 * SUBLANES, tm), col),
        pl.BlockSpec((tm, MLA_HEADS * HEAD_BLOCK), row),
        pl.BlockSpec((tm, MLA_HEADS * HEAD_BLOCK), row),
        vt_spec,
    )
    in_specs = [pl.BlockSpec((tm, d), row), pl.BlockSpec((tm, LANES), row), pl.BlockSpec((tm, LANES), row)]
    in_specs += [of_layer(p) for p in params]
    return pl.pallas_call(
        functools.partial(_in_proj_kernel, tiles_per_batch=tiles_per_batch, chunk=chunk, sub=min(tm, 256)),
        out_shape=out_shape, grid=(nt,), in_specs=in_specs, out_specs=out_specs,
        scratch_shapes=[pltpu.VMEM((tm + CONV_HALO, QK_COLS), F32)],
        compiler_params=_params("arbitrary"), name="in_proj",
    )(x2, cos_t, sin_t, *params)


def _mlstm_kernel(q_ref, k_ref, v_ref, gs_ref, out_ref, c_ref, m_ref, *, chunk):
    L = chunk

    @pl.when(pl.program_id(1) == 0)
    def _():
        c_ref[...] = jnp.zeros_like(c_ref)
        m_ref[...] = jnp.zeros_like(m_ref)

    b, g, big_g = gs_ref[0:SUBLANES, :], gs_ref[SUBLANES:2 * SUBLANES, :], gs_ref[2 * SUBLANES:, :]
    m_prev = jnp.concatenate([m_ref[...]] * (L // LANES), axis=1)
    big_m = jnp.maximum(big_g, m_prev)
    inter_w = jnp.exp(m_prev - big_m)
    floor = jnp.exp(-(b + big_m))
    m_end = big_m[:, L - 1:L]
    w_state = jnp.exp(g - m_end)
    m_ref[...] = jnp.broadcast_to(b[:, L - 1:L] + m_end, m_ref.shape)
    stats = [big_m, inter_w, floor, w_state]
    pad = jnp.zeros((LANES - len(stats) * SUBLANES, L), F32)
    cols = jnp.concatenate(stats + [pad], axis=0).T

    lane = lax.broadcasted_iota(jnp.int32, (L, LANES), 1)
    ones_blk = jnp.where(lane == 0, 1.0, 0.0).astype(BF16)
    rb = min(L, 256)
    row_blocks = [(slice(r, r + rb), slice(0, r + rb)) for r in range(0, L, rb)]

    for h in range(ML_HEADS):
        pair, lo = h // 2, (h % 2) * ML_QK
        blk = slice(pair * LANES, (pair + 1) * LANES)
        mine = (lane >= lo) & (lane < lo + ML_QK)
        qm = jnp.where(mine, q_ref[:, blk], jnp.zeros((), BF16))
        k_blk = k_ref[:, blk]
        v_ext = jnp.concatenate([v_ref[:, h * ML_V:(h + 1) * ML_V], ones_blk], axis=1)
        c_pair = c_ref[pair]
        c_bf = c_pair.astype(BF16)
        for rows, keys in row_blocks:
            n_keys = keys.stop
            s = _dot_nt(qm[rows], k_blk[keys])
            qpos = rows.start + lax.broadcasted_iota(jnp.int32, (rb, n_keys), 0)
            kpos = lax.broadcasted_iota(jnp.int32, (rb, n_keys), 1)
            dw = jnp.where(qpos >= kpos, jnp.exp(g[h:h + 1, keys] - cols[rows, h:h + 1]), 0.0)
            iw = cols[rows, SUBLANES + h:SUBLANES + h + 1]
            num = _dot((s * dw).astype(BF16), v_ext[keys]) + iw * _dot(qm[rows], c_bf)
            den = num[:, ML_V:ML_V + 1]
            hh = num[:, :ML_V] / jnp.maximum(jnp.abs(den), cols[rows, 2 * SUBLANES + h:2 * SUBLANES + h + 1])
            out_ref[rows, h * ML_V:(h + 1) * ML_V] = hh.astype(out_ref.dtype)

        kw = (k_blk.astype(F32) * cols[:, 3 * SUBLANES + h:3 * SUBLANES + h + 1]).astype(BF16)
        upd = _dot_tn(kw, v_ext)
        rows = slice(lo, lo + ML_QK)
        c_ref[pair, rows, :] = inter_w[h:h + 1, L - 1:L] * c_pair[rows, :] + upd[rows, :]


def _mlstm(q, k, v, gs, *, batch, seq, chunk):
    t = q.shape[0]
    nc = seq // chunk
    row = lambda b, c: (b * nc + c, 0)
    return pl.pallas_call(
        functools.partial(_mlstm_kernel, chunk=chunk),
        out_shape=jax.ShapeDtypeStruct((t, ML_WIDTH), BF16),
        grid=(batch, nc),
        in_specs=[pl.BlockSpec((chunk, ML_QK_WIDTH), row), pl.BlockSpec((chunk, ML_QK_WIDTH), row),
                  pl.BlockSpec((chunk, ML_WIDTH), row),
                  pl.BlockSpec((3 * SUBLANES, chunk), lambda b, c: (0, b * nc + c))],
        out_specs=pl.BlockSpec((chunk, ML_WIDTH), row),
        scratch_shapes=[pltpu.VMEM((ML_HEADS // 2, LANES, 2 * LANES), F32),
                        pltpu.VMEM((SUBLANES, LANES), F32)],
        compiler_params=_params("parallel", "arbitrary"), name="mlstm",
    )(q, k, v, gs)


def _attn_kernel(q_ref, k_ref, vt_ref, out_ref, s_sc, smax_sc, m_sc, acc_sc, *, tb, hp):
    last = pl.program_id(2)
    heads = range(hp)
    ones_rows = jnp.where(lax.broadcasted_iota(jnp.int32, (16, tb), 0) == 0, 1.0, 0.0).astype(BF16)

    def produce(j, slot, h, diagonal):
        kb = k_ref[pl.ds(pl.multiple_of(j * tb, tb), tb), h * HEAD_BLOCK:(h + 1) * HEAD_BLOCK]
        s = _dot_nt(kb, q_ref[:, h * HEAD_BLOCK:(h + 1) * HEAD_BLOCK])
        if diagonal:
            key = lax.broadcasted_iota(jnp.int32, s.shape, 0)
            qry = lax.broadcasted_iota(jnp.int32, s.shape, 1)
            s = jnp.where(key <= qry, s, -jnp.inf)
        s_sc[slot, h] = s
        smax_sc[slot, h] = jnp.max(s, axis=0, keepdims=True)

    def consume(j, slot, h):
        m = m_sc[h]
        m_new = jnp.maximum(m, smax_sc[slot, h])
        p = jnp.exp2(s_sc[slot, h] - m_new).astype(BF16)
        v_ext = jnp.concatenate([vt_ref[j, h * MLA_V:(h + 1) * MLA_V, :], ones_rows], axis=0)
        acc_sc[h] = jnp.exp2(m - m_new) * acc_sc[h] + _dot(v_ext, p)
        m_sc[h] = m_new

    def stage(j, slot, next_is_diagonal):
        for h in heads:
            produce(j + 1, 1 - slot, h, next_is_diagonal)
            consume(j, slot, h)

    def finish(slot):
        for h in heads:
            consume(last, slot, h)
        o_t = jnp.concatenate([acc_sc[h, :MLA_V, :] / acc_sc[h, MLA_V:MLA_V + 1, :] for h in heads], axis=0)
        out_ref[...] = o_t.T.astype(out_ref.dtype)

    m_sc[...] = jnp.full(m_sc.shape, -jnp.inf, F32)
    acc_sc[...] = jnp.zeros(acc_sc.shape, F32)

    @pl.when(last == 0)
    def _():
        for h in heads:
            produce(0, 0, h, True)
        finish(0)

    @pl.when(last > 0)
    def _():
        for h in heads:
            produce(0, 0, h, False)

        def stages(group):
            def body(t, carry):
                for u in range(group):
                    stage(group * t + u, u % 2, False)
                return carry
            return body

        pairs = (last - 1) // 2
        lax.fori_loop(0, pairs // 4, stages(8), 0)
        lax.fori_loop(pairs // 4 * 2, pairs // 2, stages(4), 0)
        lax.fori_loop(pairs // 2 * 2, pairs, stages(2), 0)

        @pl.when(last % 2 == 1)
        def _():
            stage(last - 1, 0, True)
            finish(1)

        @pl.when(last % 2 == 0)
        def _():
            stage(last - 2, 0, False)
            stage(last - 1, 1, True)
            finish(0)


def _attention(q, k, vt, *, batch, seq, tb, hp):
    t = q.shape[0]
    nb = seq // tb
    return pl.pallas_call(
        functools.partial(_attn_kernel, tb=tb, hp=hp),
        out_shape=jax.ShapeDtypeStruct((t, MLA_WIDTH), BF16),
        grid=(batch, MLA_HEADS // hp, nb),
        in_specs=[pl.BlockSpec((tb, hp * HEAD_BLOCK), lambda b, p, i: (b * nb + i, p)),
                  pl.BlockSpec((seq, hp * HEAD_BLOCK), lambda b, p, i: (b, p)),
                  pl.BlockSpec((None, nb, hp * MLA_V, tb), lambda b, p, i: (b, 0, p, 0))],
        out_specs=pl.BlockSpec((tb, hp * MLA_V), lambda b, p, i: (b * nb + i, p)),
        scratch_shapes=[pltpu.VMEM((2, hp, tb, tb), F32), pltpu.VMEM((2, hp, 1, tb), F32),
                        pltpu.VMEM((hp, 1, tb), F32), pltpu.VMEM((hp, MLA_V + 16, tb), F32)],
        compiler_params=_params("parallel", "parallel", "arbitrary"), name="attn",
    )(q, k, vt)


def _mix_ffn_kernel(hml_ref, o_ref, hmla_ref, x_ref, hn_ref, wa_ref, wb_ref, gpost_ref, gpre_ref,
                    wu_ref, wd_ref, gmlp_ref, out_ref, m_sc, acc_sc, *, sub):
    j = pl.program_id(1)

    def mlp_slice(m):
        h = jnp.square(jnp.maximum(_dot(m, wu_ref[...]), 0.0)).astype(BF16)
        return _dot(h, wd_ref[...])

    @pl.when(j == 0)
    def _():
        blocks = [slice(r, r + sub) for r in range(0, out_ref.shape[0], sub)]
        mixes = []
        for rows in blocks:
            heads = []
            for h in range(ML_HEADS):
                hs = slice(h * ML_V, (h + 1) * ML_V)
                hh = _rms(hml_ref[rows, hs].astype(F32), hn_ref[:, hs]) * jax.nn.sigmoid(o_ref[rows, hs].astype(F32))
                heads.append(hh.astype(BF16))
            mixes.append(_dot(jnp.concatenate(heads, axis=1), wa_ref[...]) + _dot(hmla_ref[rows, :], wb_ref[...]))
        for rows, mix in zip(blocks, mixes):
            x1 = x_ref[rows, :] + _rms(mix, gpost_ref[...])
            out_ref[rows, :] = x1
            m = _rms(x1, gpre_ref[...]).astype(m_sc.dtype)
            m_sc[rows, :] = m
            acc_sc[rows, :] = mlp_slice(m)

    last = pl.num_programs(1) - 1

    @pl.when((j > 0) & (j < last))
    def _():
        acc_sc[...] += mlp_slice(m_sc[...])

    @pl.when(j == last)
    def _():
        out_ref[...] += _rms(acc_sc[...] + mlp_slice(m_sc[...]), gmlp_ref[...])


def _mix_ffn(hml, o, hmla, x2, params, layer, *, tm, tf):
    hn, wo, gpost, gpre, wu, wd, gmlp = params
    t, d = x2.shape
    dff = wu.shape[2]
    row = lambda i, j: (i, 0)
    vec = lambda arr: pl.BlockSpec((None,) + arr.shape[1:], lambda i, j: (layer, 0, 0))
    return pl.pallas_call(
        functools.partial(_mix_ffn_kernel, sub=min(tm, 256)),
        out_shape=jax.ShapeDtypeStruct((t, d), F32),
        grid=(t // tm, dff // tf),
        in_specs=[pl.BlockSpec((tm, ML_WIDTH), row), pl.BlockSpec((tm, ML_WIDTH), row),
                  pl.BlockSpec((tm, MLA_WIDTH), row), pl.BlockSpec((tm, d), row),
                  vec(hn),
                  pl.BlockSpec((None, ML_WIDTH, d), lambda i, j: (layer, 0, 0)),
                  pl.BlockSpec((None, MLA_WIDTH, d), lambda i, j: (layer, 1, 0)),
                  vec(gpost), vec(gpre),
                  pl.BlockSpec((None, d, tf), lambda i, j: (layer, 0, j)),
                  pl.BlockSpec((None, tf, d), lambda i, j: (layer, j, 0)), vec(gmlp)],
        out_specs=pl.BlockSpec((tm, d), row),
        scratch_shapes=[pltpu.VMEM((tm, d), BF16), pltpu.VMEM((tm, d), F32)],
        compiler_params=_params("parallel", "arbitrary"), name="mix_ffn",
    )(hml, o, hmla, x2, hn, wo, wo, gpost, gpre, wu, wd, gmlp)


def _layout_in_proj(w_in, b_gates):
    depth, d, _ = w_in.shape
    o0 = QK_COLS
    o1 = o0 + ML_WIDTH
    o2 = o1 + ML_WIDTH
    o3 = o2 + 2 * ML_HEADS
    o4 = o3 + MLA_Q_RANK
    o5 = o4 + MLA_KV_RANK
    gates, k_r = w_in[..., o2:o3], w_in[..., o5:]
    pad = jnp.zeros((depth, d, LANES - ROPE_LANE1), w_in.dtype)
    kr_blk = jnp.concatenate([-k_r[..., HALF_ROPE:], k_r[..., :HALF_ROPE], pad, k_r, pad], axis=2)
    wmla = jnp.concatenate([w_in[..., o3:o5], kr_blk], axis=2).astype(BF16)
    gates_t = jnp.swapaxes(gates, 1, 2)
    zrows = jnp.zeros((depth, SUBLANES - ML_HEADS, d), w_in.dtype)
    wg = jnp.concatenate([gates_t[:, :ML_HEADS], zrows, gates_t[:, ML_HEADS:], zrows], axis=1).astype(BF16)
    zb = jnp.zeros((depth, SUBLANES - ML_HEADS), b_gates.dtype)
    gbias = jnp.concatenate([b_gates[:, :ML_HEADS], zb, b_gates[:, ML_HEADS:], zb], axis=1)[..., None]
    return w_in[..., :ML_COLS].astype(BF16), wmla, wg, gbias


def _layout_mla(w_uq, w_ukv):
    depth, rq, _ = w_uq.shape
    rkv = w_ukv.shape[1]
    wq4 = w_uq.reshape(depth, rq, MLA_HEADS, MLA_NOPE + MLA_ROPE)
    nope, rope = wq4[..., :MLA_NOPE], wq4[..., MLA_NOPE:]
    pad = jnp.zeros((depth, rq, MLA_HEADS, HEAD_BLOCK - ROPE_LANE1 - HALF_ROPE), w_uq.dtype)
    wq = jnp.concatenate([nope, rope, rope[..., :HALF_ROPE], pad], axis=3)
    wq = wq.reshape(depth, rq, MLA_HEADS * HEAD_BLOCK).astype(BF16)
    wkv4 = w_ukv.reshape(depth, rkv, MLA_HEADS, MLA_NOPE + MLA_V)
    kpad = jnp.zeros((depth, rkv, MLA_HEADS, HEAD_BLOCK - MLA_NOPE), w_ukv.dtype)
    wk = jnp.concatenate([wkv4[..., :MLA_NOPE], kpad], axis=3).reshape(depth, rkv, MLA_HEADS * HEAD_BLOCK).astype(BF16)
    wvt = jnp.swapaxes(wkv4[..., MLA_NOPE:].reshape(depth, rkv, MLA_WIDTH), 1, 2).astype(BF16)
    return wq, wk, wvt


def kernel(x, positions, norm_pre_mix, w_in, b_gates, conv_w, conv_b, ml_head_norm, q_norm, w_uq, kv_norm, w_ukv,
           w_out, norm_post_mix, norm_pre_mlp, w_up, w_down, norm_post_mlp):
    batch, seq, d = x.shape
    depth = w_in.shape[0]
    t = batch * seq
    tm_in = min(1024, seq)
    tb = min(512, seq)
    hp = 4
    chunk = min(512, seq)
    tm_ffn = min(1024, t)
    tf = 1024

    rows = lambda v: v[:, None, :]
    win, wmla, wg, gbias = _layout_in_proj(w_in, b_gates)
    wq, wk, wvt = _layout_mla(w_uq, w_ukv)
    in_params = (rows(norm_pre_mix), win, wmla, wg, gbias, conv_w, rows(conv_b), rows(q_norm), wq, rows(kv_norm), wk, wvt)
    mix_params = (rows(ml_head_norm), w_out.astype(BF16), rows(norm_post_mix), rows(norm_pre_mlp),
                  w_up.astype(BF16), w_down.astype(BF16), rows(norm_post_mlp))

    cos_t, sin_t = _rope_tables(positions)
    x2 = x.reshape(t, d)
    for l in range(depth):
        qml, kml, v, o, gs, q, k, vt = _in_proj(x2, cos_t, sin_t, in_params, l,
                                                batch=batch, seq=seq, tm=tm_in, tk=tb, chunk=chunk)
        hml = _mlstm(qml, kml, v, gs, batch=batch, seq=seq, chunk=chunk)
        hmla = _attention(q, k, vt, batch=batch, seq=seq, tb=tb, hp=hp)
        x2 = _mix_ffn(hml, o, hmla, x2, mix_params, l, tm=tm_ffn, tf=tf)
    return x2.reshape(batch, seq, d)
```
